```python
import jax
import jax.numpy as jnp
from jax import lax
import numpy as np

D_MODEL = 1024
BATCH = 4
SEQ = 8192
DEPTH = 1

GRID_W = 64
CTX_LEN = 256
N_HEADS = 8
HEAD_DIM = 64
ATTN_WIDTH = N_HEADS * HEAD_DIM
POOL_WINDOWS = (2, 4, 8, 16)
N_POOL_GROUPS = len(POOL_WINDOWS)
POOL_WIDTH = 512
POOL_GROUP = POOL_WIDTH // N_POOL_GROUPS
GATE_WIDTH = 2 * D_MODEL
IN_COLS = 3 * ATTN_WIDTH + POOL_WIDTH + GATE_WIDTH
WIN_ROWS = 8
WIN_COLS = 16
Q_COL_BLOCK = 16
K_COL_BLOCK = Q_COL_BLOCK + WIN_COLS
ROPE_FREQS = HEAD_DIM // 4
ROPE_THETA = 10000.0
D_FF = -(-8 * D_MODEL // (3 * 256)) * 256
N_MOD = 6
EPS = 1e-6
NEG_INF = -1e30

kernel_name = 'hybrid_natten_multipool_block'


def rms_norm(x, g):
    xf = x.astype(jnp.float32)
    y = xf * lax.rsqrt(jnp.mean(xf * xf, axis=-1, keepdims=True) + EPS)
    return (y * g.astype(jnp.float32)).astype(x.dtype)


def modulate(x, shift, scale):
    return x * (1.0 + scale) + shift


def heads(t):
    return t.reshape(*t.shape[:-1], N_HEADS, HEAD_DIM)


def split_projection(proj):
    a = ATTN_WIDTH
    q = proj[..., :a]
    k = proj[..., a:2 * a]
    v = proj[..., 2 * a:3 * a]
    p = proj[..., 3 * a:3 * a + POOL_WIDTH]
    gates = proj[..., 3 * a + POOL_WIDTH:]
    return q, k, v, p, gates


def axial_rope_tables(n_tok):
    t = jnp.arange(n_tok)
    pos = jnp.stack([t // GRID_W, t % GRID_W], axis=-1).astype(jnp.float32)
    inv = ROPE_THETA ** (-jnp.arange(ROPE_FREQS, dtype=jnp.float32) / ROPE_FREQS)
    ang = pos[:, :, None] * inv
    return jnp.cos(ang), jnp.sin(ang)


def apply_rope(x, cos, sin):
    b, s, h, dh = x.shape
    xf = x.astype(jnp.float32).reshape(b, s, h, 2, 2, ROPE_FREQS)
    x1, x2 = xf[..., 0, :], xf[..., 1, :]
    cs, sn = cos[:, None], sin[:, None]
    out = jnp.stack([x1 * cs - x2 * sn, x2 * cs + x1 * sn], axis=-2)
    return out.reshape(b, s, h, dh).astype(x.dtype)


def neighbourhood_attention(q, k, v, k_ctx, v_ctx, rpb):
    b, s, h, dh = q.shape
    rows = s // GRID_W
    kr = min(WIN_ROWS, rows)
    n_cb = GRID_W // Q_COL_BLOCK
    scale = dh ** -0.5
    qg = q.reshape(b, rows, n_cb, Q_COL_BLOCK, h, dh)
    kg = k.reshape(b, rows, GRID_W, h, dh)
    vg = v.reshape(b, rows, GRID_W, h, dh)
    cb = jnp.arange(n_cb)
    q_cols = cb[:, None] * Q_COL_BLOCK + jnp.arange(Q_COL_BLOCK)[None, :]
    q_col0 = jnp.clip(q_cols - WIN_COLS // 2, 0, GRID_W - WIN_COLS)
    k_col0 = jnp.clip(cb * Q_COL_BLOCK - WIN_COLS // 2, 0, GRID_W - K_COL_BLOCK)
    k_cols = k_col0[:, None] + jnp.arange(K_COL_BLOCK)[None, :]
    kc = k_cols[:, None, :]
    col_valid = (kc >= q_col0[..., None]) & (kc < q_col0[..., None] + WIN_COLS)
    col_idx = jnp.clip(kc - q_cols[..., None] + WIN_COLS - 1, 0, 2 * WIN_COLS - 2)
    mask = col_valid[:, :, None, :]
    n_win = kr * K_COL_BLOCK

    def row_block(r):
        r0 = jnp.clip(r - kr // 2, 0, rows - kr)
        q_r = lax.dynamic_index_in_dim(qg, r, axis=1, keepdims=False)
        k_band = lax.dynamic_slice_in_dim(kg, r0, kr, axis=1)[:, :, k_cols]
        v_band = lax.dynamic_slice_in_dim(vg, r0, kr, axis=1)[:, :, k_cols]
        row_idx = r0 + jnp.arange(kr) - r + WIN_ROWS - 1
        bias = rpb[:, row_idx[None, None, :, None], col_idx[:, :, None, :]]
        s_win = jnp.einsum('bnqhd,brnkhd->bhnqrk', q_r, k_band,
                           preferred_element_type=jnp.float32) * scale
        s_win = jnp.where(mask, s_win + bias.astype(jnp.float32), NEG_INF)
        s_ctx = jnp.einsum('bnqhd,blhd->bhnql', q_r, k_ctx,
                           preferred_element_type=jnp.float32) * scale
        scores = jnp.concatenate([s_win.reshape(b, h, n_cb, Q_COL_BLOCK, n_win), s_ctx], axis=-1)
        p = jax.nn.softmax(scores, axis=-1).astype(v.dtype)
        p_win = p[..., :n_win].reshape(b, h, n_cb, Q_COL_BLOCK, kr, K_COL_BLOCK)
        out = (jnp.einsum('bhnqrk,brnkhd->bnqhd', p_win, v_band)
               + jnp.einsum('bhnql,blhd->bnqhd', p[..., n_win:], v_ctx))
        return out.reshape(b, GRID_W, h * dh)

    out = lax.map(row_block, jnp.arange(rows))
    return jnp.moveaxis(out, 0, 1).reshape(b, s, h * dh)


def context_attention(q, k, v):
    b, l, h, dh = q.shape
    scores = jnp.einsum('blhd,bmhd->bhlm', q, k, preferred_element_type=jnp.float32) * dh ** -0.5
    p = jax.nn.softmax(scores, axis=-1).astype(v.dtype)
    return jnp.einsum('bhlm,bmhd->blhd', p, v).reshape(b, l, h * dh)


def multiscale_pool(p, pool_w, pool_scale):
    b, n, _ = p.shape
    pf = p.astype(jnp.float32).reshape(b, n, N_POOL_GROUPS, POOL_GROUP)
    csum = jnp.concatenate([jnp.zeros_like(pf[:, :1]), jnp.cumsum(pf, axis=1)], axis=1)
    t = jnp.arange(n)
    groups = []
    for g, w in enumerate(POOL_WINDOWS):
        lo = jnp.clip(t - w // 2, 0, n)
        hi = jnp.clip(t + w - w // 2, 0, n)
        cnt = (hi - lo).astype(jnp.float32)[None, :, None]
        mean = (csum[:, hi, g] - csum[:, lo, g]) / cnt
        groups.append(mean - pf[:, :, g])
    pooled = jnp.stack(groups, axis=2).astype(p.dtype)
    mixed = jnp.einsum('bngc,gcd->bngd', pooled, pool_w)
    return mixed.reshape(b, n, POOL_WIDTH) * pool_scale


def merge_branches(attn, pool, gates, b_gate, w_proj_a, w_proj_b, w_out):
    g = jax.nn.sigmoid(gates + b_gate)
    g_a, g_b = jnp.split(g, 2, axis=-1)
    return (g_a * (attn @ w_proj_a) + g_b * (pool @ w_proj_b)) @ w_out


def swiglu(h, w_up, w_down):
    gate, up = jnp.split(h @ w_up, 2, axis=-1)
    return (jax.nn.silu(gate) * up) @ w_down


def setup_inputs(seed: int = 0) -> dict:
    key = jax.random.key(seed)
    ks = jax.random.split(key, 20)

    def nrm(k, shape, s):
        return jax.random.normal(k, shape, jnp.float32) * s

    return {
        'x': nrm(ks[0], (BATCH, SEQ, D_MODEL), 1.0),
        'c': nrm(ks[1], (BATCH, D_MODEL), 1.0),
        'ctx': nrm(ks[2], (BATCH, CTX_LEN, D_MODEL), 1.0),
        'c_ctx': nrm(ks[3], (D_MODEL,), 1.0),
        'w_ada': nrm(ks[4], (DEPTH, D_MODEL, N_MOD * D_MODEL), 0.5 * D_MODEL ** -0.5),
        'b_ada': nrm(ks[5], (DEPTH, N_MOD * D_MODEL), 0.02),
        'g_pre_mix': 1.0 + nrm(ks[6], (DEPTH, D_MODEL), 0.05),
        'g_post_mix': 1.0 + nrm(ks[7], (DEPTH, D_MODEL), 0.05),
        'g_pre_ffn': 1.0 + nrm(ks[8], (DEPTH, D_MODEL), 0.05),
        'g_post_ffn': 1.0 + nrm(ks[9], (DEPTH, D_MODEL), 0.05),
        'w_in': nrm(ks[10], (DEPTH, D_MODEL, IN_COLS), D_MODEL ** -0.5),
        'b_gate': nrm(ks[11], (DEPTH, GATE_WIDTH), 0.02),
        'rpb': nrm(ks[12], (DEPTH, N_HEADS, 2 * WIN_ROWS - 1, 2 * WIN_COLS - 1), 0.02),
        'pool_w': nrm(ks[13], (DEPTH, N_POOL_GROUPS, POOL_GROUP, POOL_GROUP), POOL_GROUP ** -0.5),
        'pool_scale': 1.0 + nrm(ks[14], (DEPTH, POOL_WIDTH), 0.1),
        'w_proj_a': nrm(ks[15], (DEPTH, ATTN_WIDTH, D_MODEL), ATTN_WIDTH ** -0.5),
        'w_proj_b': nrm(ks[16], (DEPTH, POOL_WIDTH, D_MODEL), POOL_WIDTH ** -0.5),
        'w_out': nrm(ks[17], (DEPTH, D_MODEL, D_MODEL), D_MODEL ** -0.5),
        'w_up': nrm(ks[18], (DEPTH, D_MODEL, 2 * D_FF), D_MODEL ** -0.5),
        'w_down': nrm(ks[19], (DEPTH, D_FF, D_MODEL), D_FF ** -0.5),
    }


def reference(x, c, ctx, c_ctx, w_ada, b_ada, g_pre_mix, g_post_mix, g_pre_ffn, g_post_ffn,
              w_in, b_gate, rpb, pool_w, pool_scale, w_proj_a, w_proj_b, w_out, w_up, w_down):
    cos, sin = axial_rope_tables(x.shape[1])
    for layer in range(DEPTH):
        update_ctx = layer < DEPTH - 1
        mod = jax.nn.silu(c) @ w_ada[layer] + b_ada[layer]
        sh1, sc1, gt1, sh2, sc2, gt2 = jnp.split(mod[:, None, :], N_MOD, axis=-1)
        n_ctx_mod = N_MOD if update_ctx else 2
        mod_ctx = (jax.nn.silu(c_ctx) @ w_ada[layer][:, :n_ctx_mod * D_MODEL]
                   + b_ada[layer][:n_ctx_mod * D_MODEL])
        mod_ctx = jnp.split(mod_ctx, n_ctx_mod)

        h = modulate(rms_norm(x, g_pre_mix[layer]), sh1, sc1)
        h_ctx = modulate(rms_norm(ctx, g_pre_mix[layer]), mod_ctx[0], mod_ctx[1])
        q, k, v, p, gates = split_projection(h @ w_in[layer])
        if update_ctx:
            q_c, k_c, v_c, p_c, gates_c = split_projection(h_ctx @ w_in[layer])
        else:
            k_c, v_c = jnp.split(h_ctx @ w_in[layer][:, ATTN_WIDTH:3 * ATTN_WIDTH], 2, axis=-1)
        k_c, v_c = heads(k_c), heads(v_c)
        q = apply_rope(heads(q), cos, sin)
        k = apply_rope(heads(k), cos, sin)
        attn = neighbourhood_attention(q, k, heads(v), k_c, v_c, rpb[layer])
        pool = multiscale_pool(p, pool_w[layer], pool_scale[layer])
        y = merge_branches(attn, pool, gates, b_gate[layer], w_proj_a[layer], w_proj_b[layer], w_out[layer])
        x = x + gt1 * rms_norm(y, g_post_mix[layer])
        if update_ctx:
            attn_c = context_attention(heads(q_c), k_c, v_c)
            pool_c = multiscale_pool(p_c, pool_w[layer], pool_scale[layer])
            y_c = merge_branches(attn_c, pool_c, gates_c, b_gate[layer], w_proj_a[layer],
                                 w_proj_b[layer], w_out[layer])
            ctx = ctx + mod_ctx[2] * rms_norm(y_c, g_post_mix[layer])

        h = modulate(rms_norm(x, g_pre_ffn[layer]), sh2, sc2)
        x = x + gt2 * rms_norm(swiglu(h, w_up[layer], w_down[layer]), g_post_ffn[layer])
        if update_ctx:
            h_c = modulate(rms_norm(ctx, g_pre_ffn[layer]), mod_ctx[3], mod_ctx[4])
            ctx = ctx + mod_ctx[5] * rms_norm(swiglu(h_c, w_up[layer], w_down[layer]), g_post_ffn[layer])
    return x
```

```python
import functools

import jax
import jax.numpy as jnp
import numpy as np
from jax.experimental import pallas as pl
from jax.experimental.pallas import tpu as pltpu

D_MODEL = 1024
GRID_W = 64
N_HEADS = 8
HEAD_DIM = 64
ATTN_WIDTH = N_HEADS * HEAD_DIM
POOL_WINDOWS = (2, 4, 8, 16)
N_POOL_GROUPS = len(POOL_WINDOWS)
POOL_WIDTH = 512
POOL_GROUP = POOL_WIDTH // N_POOL_GROUPS
GATE_WIDTH = 2 * D_MODEL
WIN_ROWS = 8
WIN_COLS = 16
Q_COL_BLOCK = 16
K_COL_BLOCK = Q_COL_BLOCK + WIN_COLS
ROPE_FREQS = HEAD_DIM // 4
ROPE_THETA = 10000.0
N_MOD = 6
EPS = 1e-6
NEG_INF = -1e30

LANES = 128
HEAD_PAIR = LANES // HEAD_DIM
N_HEAD_PAIRS = N_HEADS // HEAD_PAIR
POOL_HALO = max(POOL_WINDOWS) // 2
Q_ROWS = 8
BAND_ROWS = 16
N_COL_BLOCKS = GRID_W // Q_COL_BLOCK
VMEM_LIMIT = 56 * 1024 * 1024


def _bf16(a):
    return a.astype(jnp.bfloat16)


def _dot(a, b):
    return jnp.dot(a, b, preferred_element_type=jnp.float32)


def _dot_nt(a, b):
    return jax.lax.dot_general(a, b, (((1,), (1,)), ((), ())),
                               preferred_element_type=jnp.float32)


def _rms(xf, g):
    return xf * jax.lax.rsqrt(jnp.mean(xf * xf, axis=-1, keepdims=True) + EPS) * g


def _sigmoid(z):
    return 1.0 / (1.0 + jnp.exp(-z))


def _mod_kernel(c_ref, w_ref, b_ref, o_ref):
    c = c_ref[...]
    s = c * _sigmoid(c)
    o_ref[...] = _dot(_bf16(s), _bf16(w_ref[...])) + b_ref[...]


def _mod_call(c8, w_ada, b_ada):
    n = w_ada.shape[1]
    bn = D_MODEL
    return pl.pallas_call(
        _mod_kernel,
        grid=(n // bn,),
        in_specs=[pl.BlockSpec((8, D_MODEL), lambda j: (0, 0)),
                  pl.BlockSpec((D_MODEL, bn), lambda j: (0, j)),
                  pl.BlockSpec((1, bn), lambda j: (0, j))],
        out_specs=pl.BlockSpec((8, bn), lambda j: (0, j)),
        out_shape=jax.ShapeDtypeStruct((8, n), jnp.float32),
        name="mod",
    )(c8, w_ada, b_ada.reshape(1, n))


def _ctx_kv_kernel(ctx_ref, mod_ref, g_ref, w_ref, k_ref, v_ref):
    mod = mod_ref[0]
    h = _rms(ctx_ref[0], g_ref[...]) * (1.0 + mod[:, D_MODEL:2 * D_MODEL]) + mod[:, :D_MODEL]
    kv = _dot(_bf16(h), w_ref[...])
    k_ref[0] = _bf16(kv[:, :ATTN_WIDTH])
    v_ref[0] = _bf16(kv[:, ATTN_WIDTH:])


def _ctx_kv_call(ctx, mod_ctx, g_pre, w_kv):
    b, l, d = ctx.shape
    out = jax.ShapeDtypeStruct((b, l, ATTN_WIDTH), jnp.bfloat16)
    return pl.pallas_call(
        _ctx_kv_kernel,
        grid=(b,),
        in_specs=[pl.BlockSpec((1, l, d), lambda i: (i, 0, 0)),
                  pl.BlockSpec((1, 1, 2 * d), lambda i: (0, 0, 0)),
                  pl.BlockSpec((1, d), lambda i: (0, 0)),
                  pl.BlockSpec((d, 2 * ATTN_WIDTH), lambda i: (0, 0))],
        out_specs=[pl.BlockSpec((1, l, ATTN_WIDTH), lambda i: (i, 0, 0))] * 2,
        out_shape=[out, out],
        name="ctx_kv",
    )(ctx, mod_ctx, g_pre, w_kv)


def _rope(a, cos, sin_signed, first_half):
    swapped = jnp.where(first_half, pltpu.roll(a, LANES - ROPE_FREQS, 1), pltpu.roll(a, ROPE_FREQS, 1))
    return a * cos + swapped * sin_signed


def _proj_kernel(x_ref, mod_ref, g_ref, w_ref, cos_ref, sin_ref, q_ref, k_ref, v_ref, p_ref):
    mod = mod_ref[0]
    h = _rms(x_ref[0], g_ref[...]) * (1.0 + mod[:, D_MODEL:2 * D_MODEL]) + mod[:, :D_MODEL]
    proj = _dot(_bf16(h), w_ref[...])
    cos = cos_ref[...]
    sin = sin_ref[...]
    lane = jax.lax.broadcasted_iota(jnp.int32, cos.shape, 1)
    first_half = (lane % (2 * ROPE_FREQS)) < ROPE_FREQS
    for j in range(N_HEAD_PAIRS):
        sl = slice(j * LANES, (j + 1) * LANES)
        q_ref[0, :, sl] = _bf16(_rope(proj[:, sl], cos, sin, first_half))
        ksl = slice(ATTN_WIDTH + j * LANES, ATTN_WIDTH + (j + 1) * LANES)
        k_ref[0, :, sl] = _bf16(_rope(proj[:, ksl], cos, sin, first_half))
    v_ref[0] = _bf16(proj[:, 2 * ATTN_WIDTH:3 * ATTN_WIDTH])
    p_ref[0] = proj[:, 3 * ATTN_WIDTH:]


def _proj_call(x, mod3, g_pre, w_qkvp, cos_t, sin_t, tile):
    b, s, d = x.shape
    n = w_qkvp.shape[1]
    act = jax.ShapeDtypeStruct((b, s, ATTN_WIDTH), jnp.bfloat16)
    act_spec = pl.BlockSpec((1, tile, ATTN_WIDTH), lambda i, t: (i, t, 0))
    return pl.pallas_call(
        _proj_kernel,
        grid=(b, s // tile),
        in_specs=[pl.BlockSpec((1, tile, d), lambda i, t: (i, t, 0)),
                  pl.BlockSpec((1, 1, N_MOD * d), lambda i, t: (i, 0, 0)),
                  pl.BlockSpec((1, d), lambda i, t: (0, 0)),
                  pl.BlockSpec((d, n), lambda i, t: (0, 0)),
                  pl.BlockSpec((tile, LANES), lambda i, t: (t, 0)),
                  pl.BlockSpec((tile, LANES), lambda i, t: (t, 0))],
        out_specs=[act_spec, act_spec, act_spec,
                   pl.BlockSpec((1, tile, POOL_WIDTH), lambda i, t: (i, t, 0))],
        out_shape=[act, act, act, jax.ShapeDtypeStruct((b, s, POOL_WIDTH), jnp.float32)],
        compiler_params=pltpu.CompilerParams(vmem_limit_bytes=VMEM_LIMIT),
        name="proj",
    )(x, mod3, g_pre, w_qkvp, cos_t, sin_t)


def _band_start(i, rows):
    return jnp.clip(i * Q_ROWS - WIN_ROWS // 2, 0, rows - BAND_ROWS)


def _key_col0(n):
    return min(max(n * Q_COL_BLOCK - WIN_COLS // 2, 0), GRID_W - K_COL_BLOCK)


def _attn_kernel(q_ref, k_ref, v_ref, kc_ref, vc_ref, bias_ref, o_ref, *, rows):
    i = pl.program_id(2)
    start = pl.multiple_of(_band_start(i, rows) * GRID_W, GRID_W)
    k_band = k_ref[0, pl.ds(start, BAND_ROWS * GRID_W), :].reshape(BAND_ROWS, GRID_W, LANES)
    v_band = v_ref[0, pl.ds(start, BAND_ROWS * GRID_W), :].reshape(BAND_ROWS, GRID_W, LANES)
    q_all = q_ref[0].reshape(Q_ROWS, GRID_W, LANES)
    kc = kc_ref[0]
    vc = vc_ref[0]
    n_q = Q_ROWS * Q_COL_BLOCK
    n_k = BAND_ROWS * K_COL_BLOCK
    first_c = jax.lax.broadcasted_iota(jnp.int32, kc.shape, 1) < HEAD_DIM
    first_k = jax.lax.broadcasted_iota(jnp.int32, (n_k, LANES), 1) < HEAD_DIM
    first_o = jax.lax.broadcasted_iota(jnp.int32, (n_q, LANES), 1) < HEAD_DIM
    zero = jnp.zeros((), jnp.bfloat16)
    for n in range(N_COL_BLOCKS):
        c0 = _key_col0(n)
        q_t = q_all[:, n * Q_COL_BLOCK:(n + 1) * Q_COL_BLOCK, :].reshape(n_q, LANES)
        k_t = k_band[:, c0:c0 + K_COL_BLOCK, :].reshape(n_k, LANES)
        v_t = v_band[:, c0:c0 + K_COL_BLOCK, :].reshape(n_k, LANES)
        out = None
        for e in range(HEAD_PAIR):
            k_e = jnp.where(first_k, k_t, zero) if e == 0 else jnp.where(first_k, zero, k_t)
            kc_e = jnp.where(first_c, kc, zero) if e == 0 else jnp.where(first_c, zero, kc)
            s_w = _dot_nt(q_t, k_e) + bias_ref[0, 0, n, e]
            s_c = _dot_nt(q_t, kc_e)
            m = jnp.maximum(jnp.max(s_w, axis=-1, keepdims=True), jnp.max(s_c, axis=-1, keepdims=True))
            p_w = jnp.exp(s_w - m)
            p_c = jnp.exp(s_c - m)
            l = jnp.sum(p_w, axis=-1, keepdims=True) + jnp.sum(p_c, axis=-1, keepdims=True)
            o = (_dot(_bf16(p_w), v_t) + _dot(_bf16(p_c), vc)) / l
            out = o if out is None else jnp.where(first_o, out, o)
        o_ref[0, :, n * Q_COL_BLOCK:(n + 1) * Q_COL_BLOCK, :] = _bf16(out).reshape(Q_ROWS, Q_COL_BLOCK, LANES)


def _attn_call(q, k, v, k_c, v_c, bias):
    b, s, _ = q.shape
    rows = s // GRID_W
    n_i = rows // Q_ROWS
    l = k_c.shape[1]
    tok = Q_ROWS * GRID_W

    def row_class(i):
        return jnp.where(i == 0, 0, jnp.where(i == n_i - 1, 2, 1))

    return pl.pallas_call(
        functools.partial(_attn_kernel, rows=rows),
        grid=(b, N_HEAD_PAIRS, n_i),
        in_specs=[pl.BlockSpec((1, tok, LANES), lambda bi, hp, i: (bi, i, hp)),
                  pl.BlockSpec((1, s, LANES), lambda bi, hp, i: (bi, 0, hp)),
                  pl.BlockSpec((1, s, LANES), lambda bi, hp, i: (bi, 0, hp)),
                  pl.BlockSpec((1, l, LANES), lambda bi, hp, i: (bi, 0, hp)),
                  pl.BlockSpec((1, l, LANES), lambda bi, hp, i: (bi, 0, hp)),
                  pl.BlockSpec((1, 1, N_COL_BLOCKS, HEAD_PAIR, Q_ROWS * Q_COL_BLOCK, BAND_ROWS * K_COL_BLOCK),
                               lambda bi, hp, i: (row_class(i), hp, 0, 0, 0, 0))],
        out_specs=pl.BlockSpec((1, Q_ROWS, GRID_W, LANES), lambda bi, hp, i: (bi, i, 0, hp)),
        out_shape=jax.ShapeDtypeStruct((b, rows, GRID_W, ATTN_WIDTH), jnp.bfloat16),
        compiler_params=pltpu.CompilerParams(vmem_limit_bytes=VMEM_LIMIT),
        name="attn",
    )(q, k, v, k_c, v_c, bias)


def _attn_bias_table(rpb, rows):
    n_i = rows // Q_ROWS
    a = np.arange(Q_ROWS)[:, None, None, None]
    c = np.arange(Q_COL_BLOCK)[None, :, None, None]
    ar = np.arange(BAND_ROWS)[None, None, :, None]
    kc = np.arange(K_COL_BLOCK)[None, None, None, :]
    kr_eff = min(WIN_ROWS, rows)
    tabs = []
    for i in (0, 1, n_i - 1):
        start = int(np.clip(i * Q_ROWS - WIN_ROWS // 2, 0, rows - BAND_ROWS))
        qr = i * Q_ROWS + a
        kr = start + ar
        r0 = np.clip(qr - kr_eff // 2, 0, rows - kr_eff)
        valid_r = (kr >= r0) & (kr < r0 + kr_eff)
        ridx = np.clip(kr - qr + WIN_ROWS - 1, 0, 2 * WIN_ROWS - 2)
        per_n = []
        for n in range(N_COL_BLOCKS):
            qc = n * Q_COL_BLOCK + c
            kcg = _key_col0(n) + kc
            q_col0 = np.clip(qc - WIN_COLS // 2, 0, GRID_W - WIN_COLS)
            valid_c = (kcg >= q_col0) & (kcg < q_col0 + WIN_COLS)
            cidx = np.clip(kcg - qc + WIN_COLS - 1, 0, 2 * WIN_COLS - 2)
            shape = (Q_ROWS, Q_COL_BLOCK, BAND_ROWS, K_COL_BLOCK)
            flat = (np.broadcast_to(ridx, shape) * (2 * WIN_COLS - 1) + np.broadcast_to(cidx, shape)).reshape(-1)
            valid = np.broadcast_to(valid_r & valid_c, shape).reshape(-1)
            per_n.append((flat, valid))
        tabs.append(per_n)
    flat = np.stack([np.stack([t[0] for t in per_n]) for per_n in tabs])
    valid = np.stack([np.stack([t[1] for t in per_n]) for per_n in tabs])
    rpb_flat = rpb.reshape(N_HEADS, -1)
    g = jnp.take(rpb_flat, jnp.asarray(flat.reshape(-1), jnp.int32), axis=1)
    g = g.reshape(N_HEAD_PAIRS, HEAD_PAIR, 3, N_COL_BLOCKS, Q_ROWS * Q_COL_BLOCK, BAND_ROWS * K_COL_BLOCK)
    g = jnp.where(jnp.asarray(valid.reshape(3, N_COL_BLOCKS, Q_ROWS * Q_COL_BLOCK, BAND_ROWS * K_COL_BLOCK)),
                  g, NEG_INF)
    return jnp.transpose(g, (2, 0, 3, 1, 4, 5))


def _merge_kernel(x_ref, mod_ref, gpre_ref, gpost_ref, attn_ref, p_ref, pprev_ref, pnext_ref,
                  wg_ref, bg_ref, pw_ref, ps_ref, wa_ref, wb_ref, wo_ref, o_ref, ext_ref, *, seq):
    t = pl.program_id(1)
    tile = x_ref.shape[1]
    mod = mod_ref[0]
    x = x_ref[0]
    h = _rms(x, gpre_ref[...]) * (1.0 + mod[:, D_MODEL:2 * D_MODEL]) + mod[:, :D_MODEL]
    gates = _sigmoid(_dot(_bf16(h), wg_ref[...]) + bg_ref[...])

    halo = POOL_HALO
    first = t == 0
    last = t == pl.num_programs(1) - 1
    ext_ref[0:halo, :] = jnp.where(first, 0.0, pprev_ref[0, 0])
    ext_ref[halo:halo + tile, :] = p_ref[0]
    ext_ref[halo + tile:2 * halo + tile, :] = jnp.where(last, 0.0, pnext_ref[0, 0])
    tok = t * tile + jax.lax.broadcasted_iota(jnp.int32, (tile, 1), 0)
    pooled = []
    for g, w in enumerate(POOL_WINDOWS):
        sl = slice(g * POOL_GROUP, (g + 1) * POOL_GROUP)
        acc = ext_ref[halo - w // 2:halo - w // 2 + tile, sl]
        for d in range(-w // 2 + 1, w - w // 2):
            acc = acc + ext_ref[halo + d:halo + d + tile, sl]
        cnt = (jnp.minimum(tok + (w - w // 2), seq) - jnp.maximum(tok - w // 2, 0)).astype(jnp.float32)
        pooled.append(acc / cnt - ext_ref[halo:halo + tile, sl])
    mixed = [_dot(_bf16(pooled[g]), pw_ref[g]) for g in range(N_POOL_GROUPS)]
    pool = jnp.concatenate(mixed, axis=-1) * ps_ref[...]

    ya = _dot(attn_ref[0], wa_ref[...])
    yb = _dot(_bf16(pool), wb_ref[...])
    y = _dot(_bf16(gates[:, :D_MODEL] * ya + gates[:, D_MODEL:] * yb), wo_ref[...])
    o_ref[0] = x + mod[:, 2 * D_MODEL:3 * D_MODEL] * _rms(y, gpost_ref[...])


def _merge_call(x, mod3, g_pre, g_post, attn, p, w_g, b_gate, pool_w, pool_scale, w_a, w_b, w_o, tile):
    b, s, d = x.shape
    halo = POOL_HALO
    p4 = p.reshape(b, s // halo, halo, POOL_WIDTH)
    n_halo = s // halo
    per = tile // halo

    def const(shape):
        return pl.BlockSpec(shape, lambda i, t: (0,) * len(shape))

    return pl.pallas_call(
        functools.partial(_merge_kernel, seq=s),
        grid=(b, s // tile),
        in_specs=[pl.BlockSpec((1, tile, d), lambda i, t: (i, t, 0)),
                  pl.BlockSpec((1, 1, N_MOD * d), lambda i, t: (i, 0, 0)),
                  const((1, d)), const((1, d)),
                  pl.BlockSpec((1, tile, ATTN_WIDTH), lambda i, t: (i, t, 0)),
                  pl.BlockSpec((1, tile, POOL_WIDTH), lambda i, t: (i, t, 0)),
                  pl.BlockSpec((1, 1, halo, POOL_WIDTH), lambda i, t: (i, jnp.maximum(t * per - 1, 0), 0, 0)),
                  pl.BlockSpec((1, 1, halo, POOL_WIDTH),
                               lambda i, t: (i, jnp.minimum((t + 1) * per, n_halo - 1), 0, 0)),
                  const((d, GATE_WIDTH)), const((1, GATE_WIDTH)),
                  const((N_POOL_GROUPS, POOL_GROUP, POOL_GROUP)), const((1, POOL_WIDTH)),
                  const((ATTN_WIDTH, d)), const((POOL_WIDTH, d)), const((d, d))],
        out_specs=pl.BlockSpec((1, tile, d), lambda i, t: (i, t, 0)),
        out_shape=jax.ShapeDtypeStruct((b, s, d), jnp.float32),
        scratch_shapes=[pltpu.VMEM((tile + 2 * halo, POOL_WIDTH), jnp.float32)],
        compiler_params=pltpu.CompilerParams(vmem_limit_bytes=VMEM_LIMIT),
        name="merge",
    )(x, mod3, g_pre, g_post, attn, p, p4, p4, w_g, b_gate, pool_w, pool_scale, w_a, w_b, w_o)


def _ffn_kernel(x_ref, mod_ref, gpre_ref, gpost_ref, wu_ref, wd_ref, o_ref, *, d_ff, chunk):
    mod = mod_ref[0]
    x = x_ref[0]
    h = _bf16(_rms(x, gpre_ref[...]) * (1.0 + mod[:, 4 * D_MODEL:5 * D_MODEL]) + mod[:, 3 * D_MODEL:4 * D_MODEL])
    acc = None
    for c0 in range(0, d_ff, chunk):
        gate = _dot(h, wu_ref[:, c0:c0 + chunk])
        up = _dot(h, wu_ref[:, d_ff + c0:d_ff + c0 + chunk])
        a = _bf16(gate * _sigmoid(gate) * up)
        part = _dot(a, wd_ref[c0:c0 + chunk, :])
        acc = part if acc is None else acc + part
    o_ref[0] = x + mod[:, 5 * D_MODEL:] * _rms(acc, gpost_ref[...])


def _ffn_call(x, mod3, g_pre, g_post, w_up, w_down, tile):
    b, s, d = x.shape
    d_ff = w_down.shape[0]
    chunk = 2 * LANES
    assert d_ff % chunk == 0

    def const(shape):
        return pl.BlockSpec(shape, lambda i, t: (0,) * len(shape))

    return pl.pallas_call(
        functools.partial(_ffn_kernel, d_ff=d_ff, chunk=chunk),
        grid=(b, s // tile),
        in_specs=[pl.BlockSpec((1, tile, d), lambda i, t: (i, t, 0)),
                  pl.BlockSpec((1, 1, N_MOD * d), lambda i, t: (i, 0, 0)),
                  const((1, d)), const((1, d)),
                  const((d, 2 * d_ff)), const((d_ff, d))],
        out_specs=pl.BlockSpec((1, tile, d), lambda i, t: (i, t, 0)),
        out_shape=jax.ShapeDtypeStruct((b, s, d), jnp.float32),
        compiler_params=pltpu.CompilerParams(vmem_limit_bytes=VMEM_LIMIT),
        name="ffn",
    )(x, mod3, g_pre, g_post, w_up, w_down)


def _rope_tables(n_tok):
    t = jnp.arange(n_tok)
    pos = jnp.stack([t // GRID_W, t % GRID_W], axis=-1).astype(jnp.float32)
    inv = ROPE_THETA ** (-jnp.arange(ROPE_FREQS, dtype=jnp.float32) / ROPE_FREQS)
    ang = pos[:, :, None] * inv
    cos, sin = jnp.cos(ang), jnp.sin(ang)
    cos_h = jnp.concatenate([cos[:, 0], cos[:, 0], cos[:, 1], cos[:, 1]], axis=-1)
    sin_h = jnp.concatenate([-sin[:, 0], sin[:, 0], -sin[:, 1], sin[:, 1]], axis=-1)
    return jnp.tile(cos_h, (1, HEAD_PAIR)), jnp.tile(sin_h, (1, HEAD_PAIR))


def kernel(x, c, ctx, c_ctx, w_ada, b_ada, g_pre_mix, g_post_mix, g_pre_ffn, g_post_ffn, w_in, b_gate, rpb,
           pool_w, pool_scale, w_proj_a, w_proj_b, w_out, w_up, w_down):
    b, s, d = x.shape
    depth = w_ada.shape[0]
    assert depth == 1, "context stream updates are only needed for depth > 1"
    rows = s // GRID_W
    a = ATTN_WIDTH

    c8 = jnp.zeros((8, d), jnp.float32).at[:b].set(c).at[b].set(c_ctx)
    mod = _mod_call(c8, w_ada[0], b_ada[0])
    mod3 = mod[:b].reshape(b, 1, N_MOD * d)
    mod_ctx = mod[b:b + 1, :2 * d].reshape(1, 1, 2 * d)

    w = w_in[0]
    w_q = w[:, :a] * (HEAD_DIM ** -0.5)
    w_qkvp = _bf16(jnp.concatenate([w_q, w[:, a:3 * a + POOL_WIDTH]], axis=1))
    w_kv_ctx = _bf16(w[:, a:3 * a])
    w_g = _bf16(w[:, 3 * a + POOL_WIDTH:])

    g_pre = g_pre_mix[0].reshape(1, d)
    k_c, v_c = _ctx_kv_call(ctx, mod_ctx, g_pre, w_kv_ctx)
    cos_t, sin_t = _rope_tables(s)
    q, k, v, p = _proj_call(x, mod3, g_pre, w_qkvp, cos_t, sin_t, tile=512)
    bias = _attn_bias_table(rpb[0], rows)
    attn = _attn_call(q, k, v, k_c, v_c, bias).reshape(b, s, a)
    x1 = _merge_call(x, mod3, g_pre, g_post_mix[0].reshape(1, d), attn, p, w_g, b_gate[0].reshape(1, -1),
                     _bf16(pool_w[0]), pool_scale[0].reshape(1, -1), _bf16(w_proj_a[0]), _bf16(w_proj_b[0]),
                     _bf16(w_out[0]), tile=512)
    return _ffn_call(x1, mod3, g_pre_ffn[0].reshape(1, d), g_post_ffn[0].reshape(1, d),
                     _bf16(w_up[0]), _bf16(w_down[0]), tile=512)
```

```python
import functools

import jax
import jax.numpy as jnp
from jax.experimental import pallas as pl
from jax.experimental.pallas import tpu as pltpu

D_MODEL = 1024
GRID_W = 64
N_HEADS = 8
HEAD_DIM = 64
ATTN_WIDTH = N_HEADS * HEAD_DIM
POOL_WINDOWS = (2, 4, 8, 16)
N_POOL_GROUPS = len(POOL_WINDOWS)
POOL_WIDTH = 512
POOL_GROUP = POOL_WIDTH // N_POOL_GROUPS
GATE_WIDTH = 2 * D_MODEL
WIN_ROWS = 8
WIN_COLS = 16
Q_COL_BLOCK = 16
K_COL_BLOCK = Q_COL_BLOCK + WIN_COLS
ROPE_FREQS = HEAD_DIM // 4
ROPE_THETA = 10000.0
N_MOD = 6
EPS = 1e-6
NEG_INF = -1e30

LANES = 128
HEAD_PAIR = LANES // HEAD_DIM
N_HEAD_PAIRS = N_HEADS // HEAD_PAIR
POOL_HALO = max(POOL_WINDOWS) // 2
Q_ROWS = 8
BAND_ROWS = 16
N_COL_BLOCKS = GRID_W // Q_COL_BLOCK
VMEM_LIMIT = 56 * 1024 * 1024


def _bf16(a):
    return a.astype(jnp.bfloat16)


def _dot(a, b):
    return jnp.dot(a, b, preferred_element_type=jnp.float32)


def _dot_nt(a, b):
    return jax.lax.dot_general(a, b, (((1,), (1,)), ((), ())),
                               preferred_element_type=jnp.float32)


def _rms(xf, g):
    return xf * jax.lax.rsqrt(jnp.mean(xf * xf, axis=-1, keepdims=True) + EPS) * g


def _sigmoid(z):
    return 1.0 / (1.0 + jnp.exp(-z))


def _mod_kernel(c_ref, w_ref, b_ref, o_ref):
    c = c_ref[...]
    s = c * _sigmoid(c)
    o_ref[...] = _dot(_bf16(s), _bf16(w_ref[...])) + b_ref[...]


def _mod_call(c8, w_ada, b_ada):
    n = w_ada.shape[1]
    bn = D_MODEL
    return pl.pallas_call(
        _mod_kernel,
        grid=(n // bn,),
        in_specs=[pl.BlockSpec((8, D_MODEL), lambda j: (0, 0)),
                  pl.BlockSpec((D_MODEL, bn), lambda j: (0, j)),
                  pl.BlockSpec((1, bn), lambda j: (0, j))],
        out_specs=pl.BlockSpec((8, bn), lambda j: (0, j)),
        out_shape=jax.ShapeDtypeStruct((8, n), jnp.float32),
        name="mod",
    )(c8, w_ada, b_ada.reshape(1, n))


def _ctx_kv_kernel(ctx_ref, mod_ref, g_ref, w_ref, k_ref, v_ref):
    mod = mod_ref[0]
    h = _rms(ctx_ref[0], g_ref[...]) * (1.0 + mod[:, D_MODEL:2 * D_MODEL]) + mod[:, :D_MODEL]
    kv = _dot(_bf16(h), w_ref[...])
    k_ref[0] = _bf16(kv[:, :ATTN_WIDTH])
    v_ref[0] = _bf16(kv[:, ATTN_WIDTH:])


def _ctx_kv_call(ctx, mod_ctx, g_pre, w_kv):
    b, l, d = ctx.shape
    out = jax.ShapeDtypeStruct((b, l, ATTN_WIDTH), jnp.bfloat16)
    return pl.pallas_call(
        _ctx_kv_kernel,
        grid=(b,),
        in_specs=[pl.BlockSpec((1, l, d), lambda i: (i, 0, 0)),
                  pl.BlockSpec((1, 1, 2 * d), lambda i: (0, 0, 0)),
                  pl.BlockSpec((1, d), lambda i: (0, 0)),
                  pl.BlockSpec((d, 2 * ATTN_WIDTH), lambda i: (0, 0))],
        out_specs=[pl.BlockSpec((1, l, ATTN_WIDTH), lambda i: (i, 0, 0))] * 2,
        out_shape=[out, out],
        name="ctx_kv",
    )(ctx, mod_ctx, g_pre, w_kv)


def _rope(a, cos, sin_signed, first_half):
    swapped = jnp.where(first_half, pltpu.roll(a, LANES - ROPE_FREQS, 1), pltpu.roll(a, ROPE_FREQS, 1))
    return a * cos + swapped * sin_signed


def _proj_kernel(x_ref, mod_ref, g_ref, w_ref, cos_ref, sin_ref, q_ref, k_ref, v_ref, p_ref):
    mod = mod_ref[0]
    h = _rms(x_ref[0], g_ref[...]) * (1.0 + mod[:, D_MODEL:2 * D_MODEL]) + mod[:, :D_MODEL]
    proj = _dot(_bf16(h), w_ref[...])
    cos = cos_ref[...]
    sin = sin_ref[...]
    lane = jax.lax.broadcasted_iota(jnp.int32, cos.shape, 1)
    first_half = (lane % (2 * ROPE_FREQS)) < ROPE_FREQS
    for j in range(N_HEAD_PAIRS):
        sl = slice(j * LANES, (j + 1) * LANES)
        q_ref[0, :, sl] = _bf16(_rope(proj[:, sl], cos, sin, first_half))
        ksl = slice(ATTN_WIDTH + j * LANES, ATTN_WIDTH + (j + 1) * LANES)
        k_ref[0, :, sl] = _bf16(_rope(proj[:, ksl], cos, sin, first_half))
    v_ref[0] = _bf16(proj[:, 2 * ATTN_WIDTH:3 * ATTN_WIDTH])
    p_ref[0] = proj[:, 3 * ATTN_WIDTH:]


def _proj_call(x, mod3, g_pre, w_qkvp, cos_t, sin_t, tile):
    b, s, d = x.shape
    n = w_qkvp.shape[1]
    act = jax.ShapeDtypeStruct((b, s, ATTN_WIDTH), jnp.bfloat16)
    act_spec = pl.BlockSpec((1, tile, ATTN_WIDTH), lambda i, t: (i, t, 0))
    return pl.pallas_call(
        _proj_kernel,
        grid=(b, s // tile),
        in_specs=[pl.BlockSpec((1, tile, d), lambda i, t: (i, t, 0)),
                  pl.BlockSpec((1, 1, N_MOD * d), lambda i, t: (i, 0, 0)),
                  pl.BlockSpec((1, d), lambda i, t: (0, 0)),
                  pl.BlockSpec((d, n), lambda i, t: (0, 0)),
                  pl.BlockSpec((tile, LANES), lambda i, t: (t, 0)),
                  pl.BlockSpec((tile, LANES), lambda i, t: (t, 0))],
        out_specs=[act_spec, act_spec, act_spec,
                   pl.BlockSpec((1, tile, POOL_WIDTH), lambda i, t: (i, t, 0))],
        out_shape=[act, act, act, jax.ShapeDtypeStruct((b, s, POOL_WIDTH), jnp.float32)],
        compiler_params=pltpu.CompilerParams(vmem_limit_bytes=VMEM_LIMIT),
        name="proj",
    )(x, mod3, g_pre, w_qkvp, cos_t, sin_t)


def _band_start(i, rows):
    return jnp.clip(i * Q_ROWS - WIN_ROWS // 2, 0, rows - BAND_ROWS)


def _key_col0(n):
    return min(max(n * Q_COL_BLOCK - WIN_COLS // 2, 0), GRID_W - K_COL_BLOCK)


def _attn_kernel(q_ref, k_ref, v_ref, kc_ref, vc_ref, bias_ref, o_ref, *, rows):
    i = pl.program_id(2)
    start = pl.multiple_of(_band_start(i, rows) * GRID_W, GRID_W)
    k_band = k_ref[0, pl.ds(start, BAND_ROWS * GRID_W), :].reshape(BAND_ROWS, GRID_W, LANES)
    v_band = v_ref[0, pl.ds(start, BAND_ROWS * GRID_W), :].reshape(BAND_ROWS, GRID_W, LANES)
    q_all = q_ref[0].reshape(Q_ROWS, GRID_W, LANES)
    kc = kc_ref[0]
    vc = vc_ref[0]
    n_q = Q_ROWS * Q_COL_BLOCK
    n_k = BAND_ROWS * K_COL_BLOCK
    first_c = jax.lax.broadcasted_iota(jnp.int32, kc.shape, 1) < HEAD_DIM
    first_k = jax.lax.broadcasted_iota(jnp.int32, (n_k, LANES), 1) < HEAD_DIM
    first_o = jax.lax.broadcasted_iota(jnp.int32, (n_q, LANES), 1) < HEAD_DIM
    zero = jnp.zeros((), jnp.bfloat16)
    for n in range(N_COL_BLOCKS):
        c0 = _key_col0(n)
        q_t = q_all[:, n * Q_COL_BLOCK:(n + 1) * Q_COL_BLOCK, :].reshape(n_q, LANES)
        k_t = k_band[:, c0:c0 + K_COL_BLOCK, :].reshape(n_k, LANES)
        v_t = v_band[:, c0:c0 + K_COL_BLOCK, :].reshape(n_k, LANES)
        out = None
        for e in range(HEAD_PAIR):
            k_e = jnp.where(first_k, k_t, zero) if e == 0 else jnp.where(first_k, zero, k_t)
            kc_e = jnp.where(first_c, kc, zero) if e == 0 else jnp.where(first_c, zero, kc)
            s_w = _dot_nt(q_t, k_e) + bias_ref[0, 0, n, e]
            s_c = _dot_nt(q_t, kc_e)
            m = jnp.maximum(jnp.max(s_w, axis=-1, keepdims=True), jnp.max(s_c, axis=-1, keepdims=True))
            p_w = jnp.exp(s_w - m)
            p_c = jnp.exp(s_c - m)
            l = jnp.sum(p_w, axis=-1, keepdims=True) + jnp.sum(p_c, axis=-1, keepdims=True)
            o = (_dot(_bf16(p_w), v_t) + _dot(_bf16(p_c), vc)) / l
            out = o if out is None else jnp.where(first_o, out, o)
        o_ref[0, :, n * Q_COL_BLOCK:(n + 1) * Q_COL_BLOCK, :] = _bf16(out).reshape(Q_ROWS, Q_COL_BLOCK, LANES)


def _attn_call(q, k, v, k_c, v_c, bias):
    b, s, _ = q.shape
    rows = s // GRID_W
    n_i = rows // Q_ROWS
    l = k_c.shape[1]
    tok = Q_ROWS * GRID_W

    def row_class(i):
        return jnp.where(i == 0, 0, jnp.where(i == n_i - 1, 2, 1))

    return pl.pallas_call(
        functools.partial(_attn_kernel, rows=rows),
        grid=(b, N_HEAD_PAIRS, n_i),
        in_specs=[pl.BlockSpec((1, tok, LANES), lambda bi, hp, i: (bi, i, hp)),
                  pl.BlockSpec((1, s, LANES), lambda bi, hp, i: (bi, 0, hp)),
                  pl.BlockSpec((1, s, LANES), lambda bi, hp, i: (bi, 0, hp)),
                  pl.BlockSpec((1, l, LANES), lambda bi, hp, i: (bi, 0, hp)),
                  pl.BlockSpec((1, l, LANES), lambda bi, hp, i: (bi, 0, hp)),
                  pl.BlockSpec((1, 1, N_COL_BLOCKS, HEAD_PAIR, Q_ROWS * Q_COL_BLOCK, BAND_ROWS * K_COL_BLOCK),
                               lambda bi, hp, i: (row_class(i), hp, 0, 0, 0, 0))],
        out_specs=pl.BlockSpec((1, Q_ROWS, GRID_W, LANES), lambda bi, hp, i: (bi, i, 0, hp)),
        out_shape=jax.ShapeDtypeStruct((b, rows, GRID_W, ATTN_WIDTH), jnp.bfloat16),
        compiler_params=pltpu.CompilerParams(vmem_limit_bytes=VMEM_LIMIT),
        name="attn",
    )(q, k, v, k_c, v_c, bias)


N_RPB_ROWS = 2 * WIN_ROWS - 1
N_RPB_COLS = 2 * WIN_COLS - 1
BAND_LANE_GROUPS = LANES // K_COL_BLOCK
ROW_CLASS_STEPS = (0, 1, -1)


def _row_class_geometry(i, rows):
    start = min(max(i * Q_ROWS - WIN_ROWS // 2, 0), rows - BAND_ROWS)
    r0 = [min(max(i * Q_ROWS + a - WIN_ROWS // 2, 0), rows - WIN_ROWS) - start for a in range(Q_ROWS)]
    return start, start - i * Q_ROWS + WIN_ROWS - 1, r0


def _bias_kernel(rpb_ref, o_ref, toep_ref, *, rows):
    hp = pl.program_id(0)
    n = pl.program_id(1)
    c0 = jnp.clip(n * Q_COL_BLOCK - WIN_COLS // 2, 0, GRID_W - K_COL_BLOCK)
    c = jax.lax.broadcasted_iota(jnp.int32, (Q_COL_BLOCK, LANES), 0)
    lane = jax.lax.broadcasted_iota(jnp.int32, (Q_COL_BLOCK, LANES), 1)
    group = lane // K_COL_BLOCK
    qc = n * Q_COL_BLOCK + c
    kc = c0 + lane % K_COL_BLOCK
    q_col0 = jnp.clip(qc - WIN_COLS // 2, 0, GRID_W - WIN_COLS)
    col_valid = (kc >= q_col0) & (kc < q_col0 + WIN_COLS)
    cidx = jnp.where(col_valid, kc - qc + WIN_COLS - 1, -1)
    masked = jnp.full((Q_COL_BLOCK, LANES), NEG_INF, jnp.float32)
    n_i = rows // Q_ROWS
    for e in range(HEAD_PAIR):
        base = (hp * HEAD_PAIR + e) * (N_RPB_ROWS * N_RPB_COLS)
        for ri in range(N_RPB_ROWS):
            acc = masked
            for ci in range(N_RPB_COLS):
                acc = jnp.where(cidx == ci, rpb_ref[base + ri * N_RPB_COLS + ci], acc)
            toep_ref[ri] = acc
        for rc, step in enumerate(ROW_CLASS_STEPS):
            _, ridx0, r0 = _row_class_geometry(step % n_i, rows)
            for a in range(Q_ROWS):
                for g4 in range(BAND_ROWS // BAND_LANE_GROUPS):
                    slab = masked
                    for j in range(BAND_LANE_GROUPS):
                        ar = g4 * BAND_LANE_GROUPS + j
                        if 0 <= ar - r0[a] < WIN_ROWS:
                            slab = jnp.where(group == j, toep_ref[ar - a + ridx0], slab)
                    o_ref[rc, 0, 0, e, a * Q_COL_BLOCK:(a + 1) * Q_COL_BLOCK, g4 * LANES:(g4 + 1) * LANES] = slab


def _attn_bias_table(rpb, rows):
    assert rows // Q_ROWS >= 3 and rows % Q_ROWS == 0
    n_q, n_k = Q_ROWS * Q_COL_BLOCK, BAND_ROWS * K_COL_BLOCK
    n_rc = len(ROW_CLASS_STEPS)
    return pl.pallas_call(
        functools.partial(_bias_kernel, rows=rows),
        grid=(N_HEAD_PAIRS, N_COL_BLOCKS),
        in_specs=[pl.BlockSpec(memory_space=pltpu.SMEM)],
        out_specs=pl.BlockSpec((n_rc, 1, 1, HEAD_PAIR, n_q, n_k), lambda hp, n: (0, hp, n, 0, 0, 0)),
        out_shape=jax.ShapeDtypeStruct((n_rc, N_HEAD_PAIRS, N_COL_BLOCKS, HEAD_PAIR, n_q, n_k), jnp.float32),
        scratch_shapes=[pltpu.VMEM((N_RPB_ROWS, Q_COL_BLOCK, LANES), jnp.float32)],
        name="bias",
    )(rpb.reshape(-1))


def _merge_kernel(x_ref, mod_ref, gpre_ref, gpost_ref, attn_ref, p_ref, pprev_ref, pnext_ref,
                  wg_ref, bg_ref, pw_ref, ps_ref, wa_ref, wb_ref, wo_ref, o_ref, ext_ref, *, seq):
    t = pl.program_id(1)
    tile = x_ref.shape[1]
    mod = mod_ref[0]
    x = x_ref[0]
    h = _rms(x, gpre_ref[...]) * (1.0 + mod[:, D_MODEL:2 * D_MODEL]) + mod[:, :D_MODEL]
    gates = _sigmoid(_dot(_bf16(h), wg_ref[...]) + bg_ref[...])

    halo = POOL_HALO
    first = t == 0
    last = t == pl.num_programs(1) - 1
    ext_ref[0:halo, :] = jnp.where(first, 0.0, pprev_ref[0, 0])
    ext_ref[halo:halo + tile, :] = p_ref[0]
    ext_ref[halo + tile:2 * halo + tile, :] = jnp.where(last, 0.0, pnext_ref[0, 0])
    tok = t * tile + jax.lax.broadcasted_iota(jnp.int32, (tile, 1), 0)
    pooled = []
    for g, w in enumerate(POOL_WINDOWS):
        sl = slice(g * POOL_GROUP, (g + 1) * POOL_GROUP)
        acc = ext_ref[halo - w // 2:halo - w // 2 + tile, sl]
        for d in range(-w // 2 + 1, w - w // 2):
            acc = acc + ext_ref[halo + d:halo + d + tile, sl]
        cnt = (jnp.minimum(tok + (w - w // 2), seq) - jnp.maximum(tok - w // 2, 0)).astype(jnp.float32)
        pooled.append(acc / cnt - ext_ref[halo:halo + tile, sl])
    mixed = [_dot(_bf16(pooled[g]), pw_ref[g]) for g in range(N_POOL_GROUPS)]
    pool = jnp.concatenate(mixed, axis=-1) * ps_ref[...]

    ya = _dot(attn_ref[0], wa_ref[...])
    yb = _dot(_bf16(pool), wb_ref[...])
    y = _dot(_bf16(gates[:, :D_MODEL] * ya + gates[:, D_MODEL:] * yb), wo_ref[...])
    o_ref[0] = x + mod[:, 2 * D_MODEL:3 * D_MODEL] * _rms(y, gpost_ref[...])


def _merge_call(x, mod3, g_pre, g_post, attn, p, w_g, b_gate, pool_w, pool_scale, w_a, w_b, w_o, tile):
    b, s, d = x.shape
    halo = POOL_HALO
    p4 = p.reshape(b, s // halo, halo, POOL_WIDTH)
    n_halo = s // halo
    per = tile // halo

    def const(shape):
        return pl.BlockSpec(shape, lambda i, t: (0,) * len(shape))

    return pl.pallas_call(
        functools.partial(_merge_kernel, seq=s),
        grid=(b, s // tile),
        in_specs=[pl.BlockSpec((1, tile, d), lambda i, t: (i, t, 0)),
                  pl.BlockSpec((1, 1, N_MOD * d), lambda i, t: (i, 0, 0)),
                  const((1, d)), const((1, d)),
                  pl.BlockSpec((1, tile, ATTN_WIDTH), lambda i, t: (i, t, 0)),
                  pl.BlockSpec((1, tile, POOL_WIDTH), lambda i, t: (i, t, 0)),
                  pl.BlockSpec((1, 1, halo, POOL_WIDTH), lambda i, t: (i, jnp.maximum(t * per - 1, 0), 0, 0)),
                  pl.BlockSpec((1, 1, halo, POOL_WIDTH),
                               lambda i, t: (i, jnp.minimum((t + 1) * per, n_halo - 1), 0, 0)),
                  const((d, GATE_WIDTH)), const((1, GATE_WIDTH)),
                  const((N_POOL_GROUPS, POOL_GROUP, POOL_GROUP)), const((1, POOL_WIDTH)),
                  const((ATTN_WIDTH, d)), const((POOL_WIDTH, d)), const((d, d))],
        out_specs=pl.BlockSpec((1, tile, d), lambda i, t: (i, t, 0)),
        out_shape=jax.ShapeDtypeStruct((b, s, d), jnp.float32),
        scratch_shapes=[pltpu.VMEM((tile + 2 * halo, POOL_WIDTH), jnp.float32)],
        compiler_params=pltpu.CompilerParams(vmem_limit_bytes=VMEM_LIMIT),
        name="merge",
    )(x, mod3, g_pre, g_post, attn, p, p4, p4, w_g, b_gate, pool_w, pool_scale, w_a, w_b, w_o)


def _ffn_kernel(x_ref, mod_ref, gpre_ref, gpost_ref, wu_ref, wd_ref, o_ref, *, d_ff, chunk):
    mod = mod_ref[0]
    x = x_ref[0]
    h = _bf16(_rms(x, gpre_ref[...]) * (1.0 + mod[:, 4 * D_MODEL:5 * D_MODEL]) + mod[:, 3 * D_MODEL:4 * D_MODEL])
    acc = None
    for c0 in range(0, d_ff, chunk):
        gate = _dot(h, wu_ref[:, c0:c0 + chunk])
        up = _dot(h, wu_ref[:, d_ff + c0:d_ff + c0 + chunk])
        a = _bf16(gate * _sigmoid(gate) * up)
        part = _dot(a, wd_ref[c0:c0 + chunk, :])
        acc = part if acc is None else acc + part
    o_ref[0] = x + mod[:, 5 * D_MODEL:] * _rms(acc, gpost_ref[...])


def _ffn_call(x, mod3, g_pre, g_post, w_up, w_down, tile):
    b, s, d = x.shape
    d_ff = w_down.shape[0]
    chunk = 2 * LANES
    assert d_ff % chunk == 0

    def const(shape):
        return pl.BlockSpec(shape, lambda i, t: (0,) * len(shape))

    return pl.pallas_call(
        functools.partial(_ffn_kernel, d_ff=d_ff, chunk=chunk),
        grid=(b, s // tile),
        in_specs=[pl.BlockSpec((1, tile, d), lambda i, t: (i, t, 0)),
                  pl.BlockSpec((1, 1, N_MOD * d), lambda i, t: (i, 0, 0)),
                  const((1, d)), const((1, d)),
                  const((d, 2 * d_ff)), const((d_ff, d))],
        out_specs=pl.BlockSpec((1, tile, d), lambda i, t: (i, t, 0)),
        out_shape=jax.ShapeDtypeStruct((b, s, d), jnp.float32),
        compiler_params=pltpu.CompilerParams(vmem_limit_bytes=VMEM_LIMIT),
        name="ffn",
    )(x, mod3, g_pre, g_post, w_up, w_down)


def _rope_tables(n_tok):
    t = jnp.arange(n_tok)
    pos = jnp.stack([t // GRID_W, t % GRID_W], axis=-1).astype(jnp.float32)
    inv = ROPE_THETA ** (-jnp.arange(ROPE_FREQS, dtype=jnp.float32) / ROPE_FREQS)
    ang = pos[:, :, None] * inv
    cos, sin = jnp.cos(ang), jnp.sin(ang)
    cos_h = jnp.concatenate([cos[:, 0], cos[:, 0], cos[:, 1], cos[:, 1]], axis=-1)
    sin_h = jnp.concatenate([-sin[:, 0], sin[:, 0], -sin[:, 1], sin[:, 1]], axis=-1)
    return jnp.tile(cos_h, (1, HEAD_PAIR)), jnp.tile(sin_h, (1, HEAD_PAIR))


def kernel(x, c, ctx, c_ctx, w_ada, b_ada, g_pre_mix, g_post_mix, g_pre_ffn, g_post_ffn, w_in, b_gate, rpb,
           pool_w, pool_scale, w_proj_a, w_proj_b, w_out, w_up, w_down):
    b, s, d = x.shape
    depth = w_ada.shape[0]
    assert depth == 1, "context stream updates are only needed for depth > 1"
    rows = s // GRID_W
    a = ATTN_WIDTH

    c8 = jnp.zeros((8, d), jnp.float32).at[:b].set(c).at[b].set(c_ctx)
    mod = _mod_call(c8, w_ada[0], b_ada[0])
    mod3 = mod[:b].reshape(b, 1, N_MOD * d)
    mod_ctx = mod[b:b + 1, :2 * d].reshape(1, 1, 2 * d)

    w = w_in[0]
    w_q = w[:, :a] * (HEAD_DIM ** -0.5)
    w_qkvp = _bf16(jnp.concatenate([w_q, w[:, a:3 * a + POOL_WIDTH]], axis=1))
    w_kv_ctx = _bf16(w[:, a:3 * a])
    w_g = _bf16(w[:, 3 * a + POOL_WIDTH:])

    g_pre = g_pre_mix[0].reshape(1, d)
    k_c, v_c = _ctx_kv_call(ctx, mod_ctx, g_pre, w_kv_ctx)
    cos_t, sin_t = _rope_tables(s)
    q, k, v, p = _proj_call(x, mod3, g_pre, w_qkvp, cos_t, sin_t, tile=512)
    bias = _attn_bias_table(rpb[0], rows)
    attn = _attn_call(q, k, v, k_c, v_c, bias).reshape(b, s, a)
    x1 = _merge_call(x, mod3, g_pre, g_post_mix[0].reshape(1, d), attn, p, w_g, b_gate[0].reshape(1, -1),
                     _bf16(pool_w[0]), pool_scale[0].reshape(1, -1), _bf16(w_proj_a[0]), _bf16(w_proj_b[0]),
                     _bf16(w_out[0]), tile=512)
    return _ffn_call(x1, mod3, g_pre_ffn[0].reshape(1, d), g_post_ffn[0].reshape(1, d),
                     _bf16(w_up[0]), _bf16(w_down[0]), tile=512)
```

```python
import functools

import jax
import jax.numpy as jnp
from jax.experimental import pallas as pl
from jax.experimental.pallas import tpu as pltpu

D_MODEL = 1024
GRID_W = 64
N_HEADS = 8
HEAD_DIM = 64
ATTN_WIDTH = N_HEADS * HEAD_DIM
POOL_WINDOWS = (2, 4, 8, 16)
N_POOL_GROUPS = len(POOL_WINDOWS)
POOL_WIDTH = 512
POOL_GROUP = POOL_WIDTH // N_POOL_GROUPS
GATE_WIDTH = 2 * D_MODEL
WIN_ROWS = 8
WIN_COLS = 16
Q_COL_BLOCK = 16
K_COL_BLOCK = Q_COL_BLOCK + WIN_COLS
ROPE_FREQS = HEAD_DIM // 4
ROPE_THETA = 10000.0
N_MOD = 6
EPS = 1e-6
NEG_INF = -1e30

LANES = 128
HEAD_PAIR = LANES // HEAD_DIM
N_HEAD_PAIRS = N_HEADS // HEAD_PAIR
POOL_HALO = max(POOL_WINDOWS) // 2
Q_ROWS = 8
BAND_ROWS = 16
KEY_CHUNK = 128
VAL_CHUNK = 256
N_COL_BLOCKS = GRID_W // Q_COL_BLOCK
VMEM_LIMIT = 56 * 1024 * 1024


def _bf16(a):
    return a.astype(jnp.bfloat16)


def _dot(a, b):
    return jnp.dot(a, b, preferred_element_type=jnp.float32)


def _dot_nt(a, b):
    return jax.lax.dot_general(a, b, (((1,), (1,)), ((), ())),
                               preferred_element_type=jnp.float32)


def _dot_tn(a, b):
    return jax.lax.dot_general(a, b, (((0,), (0,)), ((), ())),
                               preferred_element_type=jnp.float32)


def _rms(xf, g):
    return xf * jax.lax.rsqrt(jnp.mean(xf * xf, axis=-1, keepdims=True) + EPS) * g


def _sigmoid(z):
    return 1.0 / (1.0 + jnp.exp(-z))


def _mod_kernel(c_ref, w_ref, b_ref, o_ref):
    c = c_ref[...]
    s = c * _sigmoid(c)
    o_ref[...] = _dot(_bf16(s), _bf16(w_ref[...])) + b_ref[...]


def _mod_call(c8, w_ada, b_ada):
    n = w_ada.shape[1]
    bn = D_MODEL
    return pl.pallas_call(
        _mod_kernel,
        grid=(n // bn,),
        in_specs=[pl.BlockSpec((8, D_MODEL), lambda j: (0, 0)),
                  pl.BlockSpec((D_MODEL, bn), lambda j: (0, j)),
                  pl.BlockSpec((1, bn), lambda j: (0, j))],
        out_specs=pl.BlockSpec((8, bn), lambda j: (0, j)),
        out_shape=jax.ShapeDtypeStruct((8, n), jnp.float32),
        name="mod",
    )(c8, w_ada, b_ada.reshape(1, n))


def _ctx_kv_kernel(ctx_ref, mod_ref, g_ref, w_ref, k_ref, v_ref):
    mod = mod_ref[0]
    h = _rms(ctx_ref[0], g_ref[...]) * (1.0 + mod[:, D_MODEL:2 * D_MODEL]) + mod[:, :D_MODEL]
    kv = _dot(_bf16(h), w_ref[...])
    k_ref[0] = _bf16(kv[:, :ATTN_WIDTH])
    v_ref[0] = _bf16(kv[:, ATTN_WIDTH:])


def _ctx_kv_call(ctx, mod_ctx, g_pre, w_kv):
    b, l, d = ctx.shape
    out = jax.ShapeDtypeStruct((b, l, ATTN_WIDTH), jnp.bfloat16)
    return pl.pallas_call(
        _ctx_kv_kernel,
        grid=(b,),
        in_specs=[pl.BlockSpec((1, l, d), lambda i: (i, 0, 0)),
                  pl.BlockSpec((1, 1, 2 * d), lambda i: (0, 0, 0)),
                  pl.BlockSpec((1, d), lambda i: (0, 0)),
                  pl.BlockSpec((d, 2 * ATTN_WIDTH), lambda i: (0, 0))],
        out_specs=[pl.BlockSpec((1, l, ATTN_WIDTH), lambda i: (i, 0, 0))] * 2,
        out_shape=[out, out],
        name="ctx_kv",
    )(ctx, mod_ctx, g_pre, w_kv)


def _rope(a, cos, sin_signed, first_half):
    swapped = jnp.where(first_half, pltpu.roll(a, LANES - ROPE_FREQS, 1), pltpu.roll(a, ROPE_FREQS, 1))
    return a * cos + swapped * sin_signed


def _proj_kernel(x_ref, mod_ref, g_ref, w_ref, cos_ref, sin_ref, q_ref, k_ref, v_ref, p_ref):
    mod = mod_ref[0]
    h = _rms(x_ref[0], g_ref[...]) * (1.0 + mod[:, D_MODEL:2 * D_MODEL]) + mod[:, :D_MODEL]
    proj = _dot(_bf16(h), w_ref[...])
    cos = cos_ref[...]
    sin = sin_ref[...]
    lane = jax.lax.broadcasted_iota(jnp.int32, cos.shape, 1)
    first_half = (lane % (2 * ROPE_FREQS)) < ROPE_FREQS
    for j in range(N_HEAD_PAIRS):
        sl = slice(j * LANES, (j + 1) * LANES)
        q_ref[0, :, sl] = _bf16(_rope(proj[:, sl], cos, sin, first_half))
        ksl = slice(ATTN_WIDTH + j * LANES, ATTN_WIDTH + (j + 1) * LANES)
        k_ref[0, :, sl] = _bf16(_rope(proj[:, ksl], cos, sin, first_half))
    v_ref[0] = _bf16(proj[:, 2 * ATTN_WIDTH:3 * ATTN_WIDTH])
    p_ref[0] = proj[:, 3 * ATTN_WIDTH:]


def _proj_call(x, mod3, g_pre, w_qkvp, cos_t, sin_t, tile):
    b, s, d = x.shape
    n = w_qkvp.shape[1]
    act = jax.ShapeDtypeStruct((b, s, ATTN_WIDTH), jnp.bfloat16)
    act_spec = pl.BlockSpec((1, tile, ATTN_WIDTH), lambda i, t: (i, t, 0))
    return pl.pallas_call(
        _proj_kernel,
        grid=(b, s // tile),
        in_specs=[pl.BlockSpec((1, tile, d), lambda i, t: (i, t, 0)),
                  pl.BlockSpec((1, 1, N_MOD * d), lambda i, t: (i, 0, 0)),
                  pl.BlockSpec((1, d), lambda i, t: (0, 0)),
                  pl.BlockSpec((d, n), lambda i, t: (0, 0)),
                  pl.BlockSpec((tile, LANES), lambda i, t: (t, 0)),
                  pl.BlockSpec((tile, LANES), lambda i, t: (t, 0))],
        out_specs=[act_spec, act_spec, act_spec,
                   pl.BlockSpec((1, tile, POOL_WIDTH), lambda i, t: (i, t, 0))],
        out_shape=[act, act, act, jax.ShapeDtypeStruct((b, s, POOL_WIDTH), jnp.float32)],
        compiler_params=pltpu.CompilerParams(vmem_limit_bytes=VMEM_LIMIT),
        name="proj",
    )(x, mod3, g_pre, w_qkvp, cos_t, sin_t)


def _band_start(i, rows):
    return jnp.clip(i * Q_ROWS - WIN_ROWS // 2, 0, rows - BAND_ROWS)


def _key_col0(n):
    return min(max(n * Q_COL_BLOCK - WIN_COLS // 2, 0), GRID_W - K_COL_BLOCK)


def _attn_kernel(q_ref, k_ref, v_ref, kc_ref, vc_ref, bias_ref, o_ref, *, rows):
    i = pl.program_id(1)
    start = pl.multiple_of(_band_start(i, rows) * GRID_W, GRID_W)
    n_q = Q_ROWS * Q_COL_BLOCK
    n_k = BAND_ROWS * K_COL_BLOCK
    n_l = HEAD_PAIR * n_q
    n_ctx = kc_ref.shape[1]
    first_q = jax.lax.broadcasted_iota(jnp.int32, (n_q, LANES), 1) < HEAD_DIM
    zero = jnp.zeros((), jnp.bfloat16)
    per_val = VAL_CHUNK // KEY_CHUNK
    n_key_chunks = (n_k + n_ctx) // KEY_CHUNK
    units = [(hp, n) for hp in range(N_HEAD_PAIRS) for n in range(N_COL_BLOCKS)]
    state = [dict() for _ in units]

    def band_tile(ref, hp, n):
        c0 = _key_col0(n)
        band = ref[0, pl.ds(start, BAND_ROWS * GRID_W), hp * LANES:(hp + 1) * LANES]
        return band.reshape(BAND_ROWS, GRID_W, LANES)[:, c0:c0 + K_COL_BLOCK, :].reshape(n_k, LANES)

    def score_chunk(u, j):
        hp, n = units[u]
        st = state[u]
        if j == 0:
            q_t = q_ref[0, :, hp * LANES:(hp + 1) * LANES].reshape(Q_ROWS, GRID_W, LANES)
            q_t = q_t[:, n * Q_COL_BLOCK:(n + 1) * Q_COL_BLOCK, :].reshape(n_q, LANES)
            st["q"] = jnp.concatenate([jnp.where(first_q, q_t, zero), jnp.where(first_q, zero, q_t)], axis=0)
            st["k"] = band_tile(k_ref, hp, n)
            st["s"] = []
        lo = j * KEY_CHUNK
        if lo < n_k:
            s_j = _dot_nt(st["k"][lo:lo + KEY_CHUNK], st["q"]) + bias_ref[0, hp, n, lo:lo + KEY_CHUNK, :]
        else:
            s_j = _dot_nt(kc_ref[0, lo - n_k:lo - n_k + KEY_CHUNK, hp * LANES:(hp + 1) * LANES], st["q"])
        st["s"].append(s_j)
        m_j = jnp.max(s_j.reshape(KEY_CHUNK // 8, 8, n_l), axis=0)
        st["m"] = m_j if j == 0 else jnp.maximum(st["m"], m_j)

    def value_chunk(u, jv):
        hp, n = units[u]
        st = state[u]
        if jv == 0:
            st["m"] = jnp.max(st["m"], axis=0, keepdims=True)
            st["v"] = band_tile(v_ref, hp, n)
        p_v = []
        for j in range(jv * per_val, (jv + 1) * per_val):
            p_j = jnp.exp(st["s"][j] - st["m"])
            l_j = jnp.sum(p_j.reshape(KEY_CHUNK // 8, 8, n_l), axis=0)
            st["l"] = l_j if j == 0 else st["l"] + l_j
            p_v.append(_bf16(p_j))
        lo = jv * VAL_CHUNK
        if lo < n_k:
            v_j = st["v"][lo:lo + VAL_CHUNK]
        else:
            v_j = vc_ref[0, lo - n_k:lo - n_k + VAL_CHUNK, hp * LANES:(hp + 1) * LANES]
        o_j = _dot_tn(v_j, jnp.concatenate(p_v, axis=0))
        st["o"] = o_j if jv == 0 else st["o"] + o_j

    def finish(u):
        hp, n = units[u]
        st = state[u]
        inv_l = 1.0 / jnp.sum(st["l"], axis=0, keepdims=True)
        o_t = st["o"]
        o_t = jnp.concatenate([o_t[:HEAD_DIM, :n_q] * inv_l[:, :n_q], o_t[HEAD_DIM:, n_q:] * inv_l[:, n_q:]], axis=0)
        o_ref[0, :, n * Q_COL_BLOCK:(n + 1) * Q_COL_BLOCK, hp * LANES:(hp + 1) * LANES] = (
            _bf16(o_t.T).reshape(Q_ROWS, Q_COL_BLOCK, LANES))
        st.clear()

    for j in range(n_key_chunks):
        score_chunk(0, j)
    for u in range(len(units)):
        for jv in range(n_key_chunks // per_val):
            if u + 1 < len(units):
                for j in range(jv * per_val, (jv + 1) * per_val):
                    score_chunk(u + 1, j)
            value_chunk(u, jv)
        finish(u)


def _attn_call(q, k, v, k_c, v_c, bias):
    b, s, _ = q.shape
    rows = s // GRID_W
    n_i = rows // Q_ROWS
    l = k_c.shape[1]
    tok = Q_ROWS * GRID_W

    def row_class(i):
        return jnp.where(i == 0, 0, jnp.where(i == n_i - 1, 2, 1))

    assert (BAND_ROWS * K_COL_BLOCK) % VAL_CHUNK == 0 and l % VAL_CHUNK == 0 and VAL_CHUNK % KEY_CHUNK == 0
    whole_seq = pl.Buffered(1)
    return pl.pallas_call(
        functools.partial(_attn_kernel, rows=rows),
        grid=(b, n_i),
        in_specs=[pl.BlockSpec((1, tok, ATTN_WIDTH), lambda bi, i: (bi, i, 0)),
                  pl.BlockSpec((1, s, ATTN_WIDTH), lambda bi, i: (bi, 0, 0), pipeline_mode=whole_seq),
                  pl.BlockSpec((1, s, ATTN_WIDTH), lambda bi, i: (bi, 0, 0), pipeline_mode=whole_seq),
                  pl.BlockSpec((1, l, ATTN_WIDTH), lambda bi, i: (bi, 0, 0)),
                  pl.BlockSpec((1, l, ATTN_WIDTH), lambda bi, i: (bi, 0, 0)),
                  pl.BlockSpec((1, N_HEAD_PAIRS, N_COL_BLOCKS, BAND_ROWS * K_COL_BLOCK,
                                HEAD_PAIR * Q_ROWS * Q_COL_BLOCK),
                               lambda bi, i: (row_class(i), 0, 0, 0, 0))],
        out_specs=pl.BlockSpec((1, Q_ROWS, GRID_W, ATTN_WIDTH), lambda bi, i: (bi, i, 0, 0)),
        out_shape=jax.ShapeDtypeStruct((b, rows, GRID_W, ATTN_WIDTH), jnp.bfloat16),
        compiler_params=pltpu.CompilerParams(vmem_limit_bytes=VMEM_LIMIT),
        name="attn",
    )(q, k, v, k_c, v_c, bias)


N_RPB_ROWS = 2 * WIN_ROWS - 1
N_RPB_COLS = 2 * WIN_COLS - 1
ROW_CLASS_STEPS = (0, 1, -1)


def _row_class_geometry(i, rows):
    start = min(max(i * Q_ROWS - WIN_ROWS // 2, 0), rows - BAND_ROWS)
    r0 = [min(max(i * Q_ROWS + a - WIN_ROWS // 2, 0), rows - WIN_ROWS) - start for a in range(Q_ROWS)]
    return start, start - i * Q_ROWS + WIN_ROWS - 1, r0


def _bias_kernel(rpb_ref, o_ref, toep_ref, *, rows):
    hp = pl.program_id(0)
    n = pl.program_id(1)
    c0 = jnp.clip(n * Q_COL_BLOCK - WIN_COLS // 2, 0, GRID_W - K_COL_BLOCK)
    kc = c0 + jax.lax.broadcasted_iota(jnp.int32, (K_COL_BLOCK, LANES), 0)
    lane = jax.lax.broadcasted_iota(jnp.int32, (K_COL_BLOCK, LANES), 1)
    q_row = lane // Q_COL_BLOCK
    qc = n * Q_COL_BLOCK + lane % Q_COL_BLOCK
    q_col0 = jnp.clip(qc - WIN_COLS // 2, 0, GRID_W - WIN_COLS)
    col_valid = (kc >= q_col0) & (kc < q_col0 + WIN_COLS)
    cidx = jnp.where(col_valid, kc - qc + WIN_COLS - 1, -1)
    masked = jnp.full((K_COL_BLOCK, LANES), NEG_INF, jnp.float32)
    n_i = rows // Q_ROWS
    for e in range(HEAD_PAIR):
        base = (hp * HEAD_PAIR + e) * (N_RPB_ROWS * N_RPB_COLS)
        for ri in range(N_RPB_ROWS):
            acc = masked
            for ci in range(N_RPB_COLS):
                acc = jnp.where(cidx == ci, rpb_ref[base + ri * N_RPB_COLS + ci], acc)
            toep_ref[ri] = acc
        for rc, step in enumerate(ROW_CLASS_STEPS):
            _, ridx0, r0 = _row_class_geometry(step % n_i, rows)
            for ar in range(BAND_ROWS):
                slab = masked
                for a in range(Q_ROWS):
                    if 0 <= ar - r0[a] < WIN_ROWS:
                        slab = jnp.where(q_row == a, toep_ref[ar - a + ridx0], slab)
                o_ref[rc, 0, 0, ar * K_COL_BLOCK:(ar + 1) * K_COL_BLOCK, e * LANES:(e + 1) * LANES] = slab


def _attn_bias_table(rpb, rows):
    assert rows // Q_ROWS >= 3 and rows % Q_ROWS == 0
    assert Q_ROWS * Q_COL_BLOCK == LANES
    n_q, n_k = HEAD_PAIR * Q_ROWS * Q_COL_BLOCK, BAND_ROWS * K_COL_BLOCK
    n_rc = len(ROW_CLASS_STEPS)
    return pl.pallas_call(
        functools.partial(_bias_kernel, rows=rows),
        grid=(N_HEAD_PAIRS, N_COL_BLOCKS),
        in_specs=[pl.BlockSpec(memory_space=pltpu.SMEM)],
        out_specs=pl.BlockSpec((n_rc, 1, 1, n_k, n_q), lambda hp, n: (0, hp, n, 0, 0)),
        out_shape=jax.ShapeDtypeStruct((n_rc, N_HEAD_PAIRS, N_COL_BLOCKS, n_k, n_q), jnp.float32),
        scratch_shapes=[pltpu.VMEM((N_RPB_ROWS, K_COL_BLOCK, LANES), jnp.float32)],
        name="bias",
    )(rpb.reshape(-1))


def _merge_kernel(x_ref, mod_ref, gpre_ref, gpost_ref, attn_ref, p_ref, pprev_ref, pnext_ref,
                  wg_ref, bg_ref, pw_ref, ps_ref, wa_ref, wb_ref, wo_ref, o_ref, ext_ref, *, seq):
    t = pl.program_id(1)
    tile = x_ref.shape[1]
    mod = mod_ref[0]
    x = x_ref[0]
    h = _rms(x, gpre_ref[...]) * (1.0 + mod[:, D_MODEL:2 * D_MODEL]) + mod[:, :D_MODEL]
    gates = _sigmoid(_dot(_bf16(h), wg_ref[...]) + bg_ref[...])

    halo = POOL_HALO
    first = t == 0
    last = t == pl.num_programs(1) - 1
    ext_ref[0:halo, :] = jnp.where(first, 0.0, pprev_ref[0, 0])
    ext_ref[halo:halo + tile, :] = p_ref[0]
    ext_ref[halo + tile:2 * halo + tile, :] = jnp.where(last, 0.0, pnext_ref[0, 0])
    tok = t * tile + jax.lax.broadcasted_iota(jnp.int32, (tile, 1), 0)
    pooled = []
    for g, w in enumerate(POOL_WINDOWS):
        sl = slice(g * POOL_GROUP, (g + 1) * POOL_GROUP)
        acc = ext_ref[halo - w // 2:halo - w // 2 + tile, sl]
        for d in range(-w // 2 + 1, w - w // 2):
            acc = acc + ext_ref[halo + d:halo + d + tile, sl]
        cnt = (jnp.minimum(tok + (w - w // 2), seq) - jnp.maximum(tok - w // 2, 0)).astype(jnp.float32)
        pooled.append(acc / cnt - ext_ref[halo:halo + tile, sl])
    mixed = [_dot(_bf16(pooled[g]), pw_ref[g]) for g in range(N_POOL_GROUPS)]
    pool = jnp.concatenate(mixed, axis=-1) * ps_ref[...]

    ya = _dot(attn_ref[0], wa_ref[...])
    yb = _dot(_bf16(pool), wb_ref[...])
    y = _dot(_bf16(gates[:, :D_MODEL] * ya + gates[:, D_MODEL:] * yb), wo_ref[...])
    o_ref[0] = x + mod[:, 2 * D_MODEL:3 * D_MODEL] * _rms(y, gpost_ref[...])


def _merge_call(x, mod3, g_pre, g_post, attn, p, w_g, b_gate, pool_w, pool_scale, w_a, w_b, w_o, tile):
    b, s, d = x.shape
    halo = POOL_HALO
    p4 = p.reshape(b, s // halo, halo, POOL_WIDTH)
    n_halo = s // halo
    per = tile // halo

    def const(shape):
        return pl.BlockSpec(shape, lambda i, t: (0,) * len(shape))

    return pl.pallas_call(
        functools.partial(_merge_kernel, seq=s),
        grid=(b, s // tile),
        in_specs=[pl.BlockSpec((1, tile, d), lambda i, t: (i, t, 0)),
                  pl.BlockSpec((1, 1, N_MOD * d), lambda i, t: (i, 0, 0)),
                  const((1, d)), const((1, d)),
                  pl.BlockSpec((1, tile, ATTN_WIDTH), lambda i, t: (i, t, 0)),
                  pl.BlockSpec((1, tile, POOL_WIDTH), lambda i, t: (i, t, 0)),
                  pl.BlockSpec((1, 1, halo, POOL_WIDTH), lambda i, t: (i, jnp.maximum(t * per - 1, 0), 0, 0)),
                  pl.BlockSpec((1, 1, halo, POOL_WIDTH),
                               lambda i, t: (i, jnp.minimum((t + 1) * per, n_halo - 1), 0, 0)),
                  const((d, GATE_WIDTH)), const((1, GATE_WIDTH)),
                  const((N_POOL_GROUPS, POOL_GROUP, POOL_GROUP)), const((1, POOL_WIDTH)),
                  const((ATTN_WIDTH, d)), const((POOL_WIDTH, d)), const((d, d))],
        out_specs=pl.BlockSpec((1, tile, d), lambda i, t: (i, t, 0)),
        out_shape=jax.ShapeDtypeStruct((b, s, d), jnp.float32),
        scratch_shapes=[pltpu.VMEM((tile + 2 * halo, POOL_WIDTH), jnp.float32)],
        compiler_params=pltpu.CompilerParams(vmem_limit_bytes=VMEM_LIMIT),
        name="merge",
    )(x, mod3, g_pre, g_post, attn, p, p4, p4, w_g, b_gate, pool_w, pool_scale, w_a, w_b, w_o)


def _ffn_kernel(x_ref, mod_ref, gpre_ref, gpost_ref, wu_ref, wd_ref, o_ref, *, d_ff, chunk):
    mod = mod_ref[0]
    x = x_ref[0]
    h = _bf16(_rms(x, gpre_ref[...]) * (1.0 + mod[:, 4 * D_MODEL:5 * D_MODEL]) + mod[:, 3 * D_MODEL:4 * D_MODEL])
    acc = None
    for c0 in range(0, d_ff, chunk):
        gate = _dot(h, wu_ref[:, c0:c0 + chunk])
        up = _dot(h, wu_ref[:, d_ff + c0:d_ff + c0 + chunk])
        a = _bf16(gate * _sigmoid(gate) * up)
        part = _dot(a, wd_ref[c0:c0 + chunk, :])
        acc = part if acc is None else acc + part
    o_ref[0] = x + mod[:, 5 * D_MODEL:] * _rms(acc, gpost_ref[...])


def _ffn_call(x, mod3, g_pre, g_post, w_up, w_down, tile):
    b, s, d = x.shape
    d_ff = w_down.shape[0]
    chunk = 2 * LANES
    assert d_ff % chunk == 0

    def const(shape):
        return pl.BlockSpec(shape, lambda i, t: (0,) * len(shape))

    return pl.pallas_call(
        functools.partial(_ffn_kernel, d_ff=d_ff, chunk=chunk),
        grid=(b, s // tile),
        in_specs=[pl.BlockSpec((1, tile, d), lambda i, t: (i, t, 0)),
                  pl.BlockSpec((1, 1, N_MOD * d), lambda i, t: (i, 0, 0)),
                  const((1, d)), const((1, d)),
                  const((d, 2 * d_ff)), const((d_ff, d))],
        out_specs=pl.BlockSpec((1, tile, d), lambda i, t: (i, t, 0)),
        out_shape=jax.ShapeDtypeStruct((b, s, d), jnp.float32),
        compiler_params=pltpu.CompilerParams(vmem_limit_bytes=VMEM_LIMIT),
        name="ffn",
    )(x, mod3, g_pre, g_post, w_up, w_down)


def _rope_tables(n_tok):
    t = jnp.arange(n_tok)
    pos = jnp.stack([t // GRID_W, t % GRID_W], axis=-1).astype(jnp.float32)
    inv = ROPE_THETA ** (-jnp.arange(ROPE_FREQS, dtype=jnp.float32) / ROPE_FREQS)
    ang = pos[:, :, None] * inv
    cos, sin = jnp.cos(ang), jnp.sin(ang)
    cos_h = jnp.concatenate([cos[:, 0], cos[:, 0], cos[:, 1], cos[:, 1]], axis=-1)
    sin_h = jnp.concatenate([-sin[:, 0], sin[:, 0], -sin[:, 1], sin[:, 1]], axis=-1)
    return jnp.tile(cos_h, (1, HEAD_PAIR)), jnp.tile(sin_h, (1, HEAD_PAIR))


def kernel(x, c, ctx, c_ctx, w_ada, b_ada, g_pre_mix, g_post_mix, g_pre_ffn, g_post_ffn, w_in, b_gate, rpb,
           pool_w, pool_scale, w_proj_a, w_proj_b, w_out, w_up, w_down):
    b, s, d = x.shape
    depth = w_ada.shape[0]
    assert depth == 1, "context stream updates are only needed for depth > 1"
    rows = s // GRID_W
    a = ATTN_WIDTH

    c8 = jnp.zeros((8, d), jnp.float32).at[:b].set(c).at[b].set(c_ctx)
    mod = _mod_call(c8, w_ada[0], b_ada[0])
    mod3 = mod[:b].reshape(b, 1, N_MOD * d)
    mod_ctx = mod[b:b + 1, :2 * d].reshape(1, 1, 2 * d)

    w = w_in[0]
    w_q = w[:, :a] * (HEAD_DIM ** -0.5)
    w_qkvp = _bf16(jnp.concatenate([w_q, w[:, a:3 * a + POOL_WIDTH]], axis=1))
    w_kv_ctx = _bf16(w[:, a:3 * a])
    w_g = _bf16(w[:, 3 * a + POOL_WIDTH:])

    g_pre = g_pre_mix[0].reshape(1, d)
    k_c, v_c = _ctx_kv_call(ctx, mod_ctx, g_pre, w_kv_ctx)
    cos_t, sin_t = _rope_tables(s)
    q, k, v, p = _proj_call(x, mod3, g_pre, w_qkvp, cos_t, sin_t, tile=512)
    bias = _attn_bias_table(rpb[0], rows)
    attn = _attn_call(q, k, v, k_c, v_c, bias).reshape(b, s, a)
    x1 = _merge_call(x, mod3, g_pre, g_post_mix[0].reshape(1, d), attn, p, w_g, b_gate[0].reshape(1, -1),
                     _bf16(pool_w[0]), pool_scale[0].reshape(1, -1), _bf16(w_proj_a[0]), _bf16(w_proj_b[0]),
                     _bf16(w_out[0]), tile=512)
    return _ffn_call(x1, mod3, g_pre_ffn[0].reshape(1, d), g_post_ffn[0].reshape(1, d),
                     _bf16(w_up[0]), _bf16(w_down[0]), tile=512)
```

```python
import functools

import jax
import jax.numpy as jnp
from jax.experimental import pallas as pl
from jax.experimental.pallas import tpu as pltpu

D_MODEL = 1024
GRID_W = 64
N_HEADS = 8
HEAD_DIM = 64
ATTN_WIDTH = N_HEADS * HEAD_DIM
POOL_WINDOWS = (2, 4, 8, 16)
N_POOL_GROUPS = len(POOL_WINDOWS)
POOL_WIDTH = 512
POOL_GROUP = POOL_WIDTH // N_POOL_GROUPS
GATE_WIDTH = 2 * D_MODEL
WIN_ROWS = 8
WIN_COLS = 16
Q_COL_BLOCK = 16
K_COL_BLOCK = Q_COL_BLOCK + WIN_COLS
ROPE_FREQS = HEAD_DIM // 4
ROPE_THETA = 10000.0
N_MOD = 6
EPS = 1e-6
NEG_INF = -1e30

LANES = 128
BF16_SUBLANES = 16
LOG2E = 1.4426950408889634
HEAD_PAIR = LANES // HEAD_DIM
N_HEAD_PAIRS = N_HEADS // HEAD_PAIR
POOL_HALO = max(POOL_WINDOWS) // 2
Q_ROWS = 8
BAND_ROWS = 16
KEY_CHUNK = 256
VAL_CHUNK = 256
GATE_CHUNK = 256
VALUE_LAG = 2
N_COL_BLOCKS = GRID_W // Q_COL_BLOCK
VMEM_LIMIT = 56 * 1024 * 1024


def _bf16(a):
    return a.astype(jnp.bfloat16)


def _dot(a, b):
    return jnp.dot(a, b, preferred_element_type=jnp.float32)


def _dot_nt(a, b):
    return jax.lax.dot_general(a, b, (((1,), (1,)), ((), ())),
                               preferred_element_type=jnp.float32)


def _dot_tn(a, b):
    return jax.lax.dot_general(a, b, (((0,), (0,)), ((), ())),
                               preferred_element_type=jnp.float32)


def _rms(xf, g):
    return (xf * jax.lax.rsqrt(jnp.mean(xf * xf, axis=-1, keepdims=True) + EPS)) * g


def _norm_mod(xf, g, scale, shift):
    return _rms(xf, g * (1.0 + scale)) + shift


def _sigmoid(z):
    return 1.0 / (1.0 + jnp.exp(-z))


def _mod_kernel(c_ref, w_ref, b_ref, o_ref):
    c = c_ref[...]
    s = c * _sigmoid(c)
    o_ref[...] = _dot(_bf16(s), _bf16(w_ref[...])) + b_ref[...]


def _mod_call(c8, w_ada, b_ada):
    n = w_ada.shape[1]
    bn = D_MODEL
    return pl.pallas_call(
        _mod_kernel,
        grid=(n // bn,),
        in_specs=[pl.BlockSpec((8, D_MODEL), lambda j: (0, 0)),
                  pl.BlockSpec((D_MODEL, bn), lambda j: (0, j)),
                  pl.BlockSpec((1, bn), lambda j: (0, j))],
        out_specs=pl.BlockSpec((8, bn), lambda j: (0, j)),
        out_shape=jax.ShapeDtypeStruct((8, n), jnp.float32),
        name="mod",
    )(c8, w_ada, b_ada.reshape(1, n))


def _ctx_kv_kernel(ctx_ref, mod_ref, g_ref, w_ref, k_ref, v_ref):
    mod = mod_ref[0]
    h = _norm_mod(ctx_ref[0], g_ref[...], mod[:, D_MODEL:2 * D_MODEL], mod[:, :D_MODEL])
    kv = _dot(_bf16(h), w_ref[...])
    k_ref[0] = _bf16(kv[:, :ATTN_WIDTH])
    v_ref[0] = _bf16(kv[:, ATTN_WIDTH:])


def _ctx_kv_call(ctx, mod_ctx, g_pre, w_kv):
    b, l, d = ctx.shape
    out = jax.ShapeDtypeStruct((b, l, ATTN_WIDTH), jnp.bfloat16)
    return pl.pallas_call(
        _ctx_kv_kernel,
        grid=(b,),
        in_specs=[pl.BlockSpec((1, l, d), lambda i: (i, 0, 0)),
                  pl.BlockSpec((1, 1, 2 * d), lambda i: (0, 0, 0)),
                  pl.BlockSpec((1, d), lambda i: (0, 0)),
                  pl.BlockSpec((d, 2 * ATTN_WIDTH), lambda i: (0, 0))],
        out_specs=[pl.BlockSpec((1, l, ATTN_WIDTH), lambda i: (i, 0, 0))] * 2,
        out_shape=[out, out],
        name="ctx_kv",
    )(ctx, mod_ctx, g_pre, w_kv)


def _rope(a, cos, sin_signed, first_half):
    swapped = jnp.where(first_half, pltpu.roll(a, LANES - ROPE_FREQS, 1), pltpu.roll(a, ROPE_FREQS, 1))
    return a * cos + swapped * sin_signed


def _proj_kernel(x_ref, mod_ref, g_ref, w_ref, cos_ref, sin_ref, q_ref, k_ref, v_ref, p_ref):
    mod = mod_ref[0]
    h = _norm_mod(x_ref[0], g_ref[...], mod[:, D_MODEL:2 * D_MODEL], mod[:, :D_MODEL])
    proj = _dot(_bf16(h), w_ref[...])
    cos = cos_ref[...]
    sin = sin_ref[...]
    lane = jax.lax.broadcasted_iota(jnp.int32, cos.shape, 1)
    first_half = (lane % (2 * ROPE_FREQS)) < ROPE_FREQS
    for j in range(N_HEAD_PAIRS):
        sl = slice(j * LANES, (j + 1) * LANES)
        q_ref[0, :, sl] = _bf16(_rope(proj[:, sl], cos, sin, first_half))
        ksl = slice(ATTN_WIDTH + j * LANES, ATTN_WIDTH + (j + 1) * LANES)
        k_ref[0, :, sl] = _bf16(_rope(proj[:, ksl], cos, sin, first_half))
    v_ref[0] = _bf16(proj[:, 2 * ATTN_WIDTH:3 * ATTN_WIDTH])
    p_ref[0] = proj[:, 3 * ATTN_WIDTH:]


def _proj_call(x, mod3, g_pre, w_qkvp, cos_t, sin_t, tile):
    b, s, d = x.shape
    n = w_qkvp.shape[1]
    act = jax.ShapeDtypeStruct((b, s, ATTN_WIDTH), jnp.bfloat16)
    act_spec = pl.BlockSpec((1, tile, ATTN_WIDTH), lambda i, t: (i, t, 0))
    return pl.pallas_call(
        _proj_kernel,
        grid=(b, s // tile),
        in_specs=[pl.BlockSpec((1, tile, d), lambda i, t: (i, t, 0)),
                  pl.BlockSpec((1, 1, N_MOD * d), lambda i, t: (i, 0, 0)),
                  pl.BlockSpec((1, d), lambda i, t: (0, 0)),
                  pl.BlockSpec((d, n), lambda i, t: (0, 0)),
                  pl.BlockSpec((tile, LANES), lambda i, t: (t, 0)),
                  pl.BlockSpec((tile, LANES), lambda i, t: (t, 0))],
        out_specs=[act_spec, act_spec, act_spec,
                   pl.BlockSpec((1, tile, POOL_WIDTH), lambda i, t: (i, t, 0))],
        out_shape=[act, act, act, jax.ShapeDtypeStruct((b, s, POOL_WIDTH), jnp.float32)],
        compiler_params=pltpu.CompilerParams(vmem_limit_bytes=VMEM_LIMIT),
        name="proj",
    )(x, mod3, g_pre, w_qkvp, cos_t, sin_t)


def _band_start(i, rows):
    return jnp.clip(i * Q_ROWS - WIN_ROWS // 2, 0, rows - BAND_ROWS)


def _key_col0(n):
    return min(max(n * Q_COL_BLOCK - WIN_COLS // 2, 0), GRID_W - K_COL_BLOCK)


def _attn_kernel(q_ref, k_ref, v_ref, kc_ref, vc_ref, bias_ref, o_ref, *, rows):
    i = pl.program_id(1)
    start = pl.multiple_of(_band_start(i, rows) * GRID_W, GRID_W)
    n_q = Q_ROWS * Q_COL_BLOCK
    n_k = BAND_ROWS * K_COL_BLOCK
    n_l = HEAD_PAIR * n_q
    n_ctx = kc_ref.shape[1]
    first_d = jax.lax.broadcasted_iota(jnp.int32, (LANES, n_q), 0) < HEAD_DIM
    zero = jnp.zeros((), jnp.bfloat16)
    ones_rows = jnp.ones((BF16_SUBLANES, VAL_CHUNK), jnp.bfloat16)
    per_val = VAL_CHUNK // KEY_CHUNK
    n_key_chunks = (n_k + n_ctx) // KEY_CHUNK
    units = [(hp, n) for hp in range(N_HEAD_PAIRS) for n in range(N_COL_BLOCKS)]
    state = [dict() for _ in units]

    def band_tile(ref, hp, n):
        c0 = _key_col0(n)
        band = ref[0, pl.ds(start, BAND_ROWS * GRID_W), hp * LANES:(hp + 1) * LANES]
        return band.reshape(BAND_ROWS, GRID_W, LANES)[:, c0:c0 + K_COL_BLOCK, :].reshape(n_k, LANES)

    def score_chunk(u, j):
        hp, n = units[u]
        st = state[u]
        if j == 0:
            q_t = q_ref[0, :, hp * LANES:(hp + 1) * LANES].reshape(Q_ROWS, GRID_W, LANES)
            q_t = q_t[:, n * Q_COL_BLOCK:(n + 1) * Q_COL_BLOCK, :].reshape(n_q, LANES).T
            st["q"] = jnp.concatenate([jnp.where(first_d, q_t, zero), jnp.where(first_d, zero, q_t)], axis=1)
            st["k"] = band_tile(k_ref, hp, n)
            st["s"] = []
        lo = j * KEY_CHUNK
        if lo < n_k:
            s_j = _dot(st["k"][lo:lo + KEY_CHUNK], st["q"]) + bias_ref[0, hp, n, lo:lo + KEY_CHUNK, :]
        else:
            s_j = _dot(kc_ref[0, lo - n_k:lo - n_k + KEY_CHUNK, hp * LANES:(hp + 1) * LANES], st["q"])
        st["s"].append(s_j)
        m_j = jnp.max(s_j.reshape(KEY_CHUNK // 8, 8, n_l), axis=0)
        st["m"] = m_j if j == 0 else jnp.maximum(st["m"], m_j)

    def value_chunk(u, jv):
        hp, n = units[u]
        st = state[u]
        if jv == 0:
            st["m"] = jnp.max(st["m"], axis=0, keepdims=True)
            st["v"] = band_tile(v_ref, hp, n)
        p_v = [_bf16(jnp.exp2(st["s"][j] - st["m"])) for j in range(jv * per_val, (jv + 1) * per_val)]
        lo = jv * VAL_CHUNK
        if lo < n_k:
            v_j = st["v"][lo:lo + VAL_CHUNK]
        else:
            v_j = vc_ref[0, lo - n_k:lo - n_k + VAL_CHUNK, hp * LANES:(hp + 1) * LANES]
        v_ext = jnp.concatenate([v_j.T, ones_rows], axis=0)
        o_j = _dot(v_ext, jnp.concatenate(p_v, axis=0))
        st["o"] = o_j if jv == 0 else st["o"] + o_j

    def finish(u):
        hp, n = units[u]
        st = state[u]
        o_t = st["o"]
        inv_l = 1.0 / o_t[LANES:LANES + 1, :]
        o_t = jnp.concatenate([o_t[:HEAD_DIM, :n_q] * inv_l[:, :n_q],
                               o_t[HEAD_DIM:LANES, n_q:] * inv_l[:, n_q:]], axis=0)
        o_ref[0, :, n * Q_COL_BLOCK:(n + 1) * Q_COL_BLOCK, hp * LANES:(hp + 1) * LANES] = (
            _bf16(o_t.T).reshape(Q_ROWS, Q_COL_BLOCK, LANES))
        st.clear()

    n_val_chunks = n_key_chunks // per_val
    n_slots = len(units) * n_val_chunks
    for slot in range(n_slots + n_val_chunks + VALUE_LAG):
        if slot < n_slots:
            u, jv = divmod(slot, n_val_chunks)
            for j in range(jv * per_val, (jv + 1) * per_val):
                score_chunk(u, j)
        done = slot - n_val_chunks - VALUE_LAG
        if done >= 0:
            u, jv = divmod(done, n_val_chunks)
            value_chunk(u, jv)
            if jv == n_val_chunks - 1:
                finish(u)


def _attn_call(q, k, v, k_c, v_c, bias):
    b, s, _ = q.shape
    rows = s // GRID_W
    n_i = rows // Q_ROWS
    l = k_c.shape[1]
    tok = Q_ROWS * GRID_W

    def row_class(i):
        return jnp.where(i == 0, 0, jnp.where(i == n_i - 1, 2, 1))

    assert (BAND_ROWS * K_COL_BLOCK) % VAL_CHUNK == 0 and l % VAL_CHUNK == 0 and VAL_CHUNK % KEY_CHUNK == 0
    whole_seq = pl.Buffered(1)
    return pl.pallas_call(
        functools.partial(_attn_kernel, rows=rows),
        grid=(b, n_i),
        in_specs=[pl.BlockSpec((1, tok, ATTN_WIDTH), lambda bi, i: (bi, i, 0)),
                  pl.BlockSpec((1, s, ATTN_WIDTH), lambda bi, i: (bi, 0, 0), pipeline_mode=whole_seq),
                  pl.BlockSpec((1, s, ATTN_WIDTH), lambda bi, i: (bi, 0, 0), pipeline_mode=whole_seq),
                  pl.BlockSpec((1, l, ATTN_WIDTH), lambda bi, i: (bi, 0, 0)),
                  pl.BlockSpec((1, l, ATTN_WIDTH), lambda bi, i: (bi, 0, 0)),
                  pl.BlockSpec((1, N_HEAD_PAIRS, N_COL_BLOCKS, BAND_ROWS * K_COL_BLOCK,
                                HEAD_PAIR * Q_ROWS * Q_COL_BLOCK),
                               lambda bi, i: (row_class(i), 0, 0, 0, 0))],
        out_specs=pl.BlockSpec((1, Q_ROWS, GRID_W, ATTN_WIDTH), lambda bi, i: (bi, i, 0, 0)),
        out_shape=jax.ShapeDtypeStruct((b, rows, GRID_W, ATTN_WIDTH), jnp.bfloat16),
        compiler_params=pltpu.CompilerParams(vmem_limit_bytes=VMEM_LIMIT),
        name="attn",
    )(q, k, v, k_c, v_c, bias)


N_RPB_ROWS = 2 * WIN_ROWS - 1
N_RPB_COLS = 2 * WIN_COLS - 1
ROW_CLASS_STEPS = (0, 1, -1)


def _row_class_geometry(i, rows):
    start = min(max(i * Q_ROWS - WIN_ROWS // 2, 0), rows - BAND_ROWS)
    r0 = [min(max(i * Q_ROWS + a - WIN_ROWS // 2, 0), rows - WIN_ROWS) - start for a in range(Q_ROWS)]
    return start, start - i * Q_ROWS + WIN_ROWS - 1, r0


def _bias_kernel(rpb_ref, o_ref, toep_ref, *, rows):
    hp = pl.program_id(0)
    n = pl.program_id(1)
    c0 = jnp.clip(n * Q_COL_BLOCK - WIN_COLS // 2, 0, GRID_W - K_COL_BLOCK)
    kc = c0 + jax.lax.broadcasted_iota(jnp.int32, (K_COL_BLOCK, LANES), 0)
    lane = jax.lax.broadcasted_iota(jnp.int32, (K_COL_BLOCK, LANES), 1)
    q_row = lane // Q_COL_BLOCK
    qc = n * Q_COL_BLOCK + lane % Q_COL_BLOCK
    q_col0 = jnp.clip(qc - WIN_COLS // 2, 0, GRID_W - WIN_COLS)
    col_valid = (kc >= q_col0) & (kc < q_col0 + WIN_COLS)
    cidx = jnp.where(col_valid, kc - qc + WIN_COLS - 1, -1)
    masked = jnp.full((K_COL_BLOCK, LANES), NEG_INF, jnp.float32)
    n_i = rows // Q_ROWS
    for e in range(HEAD_PAIR):
        base = (hp * HEAD_PAIR + e) * (N_RPB_ROWS * N_RPB_COLS)
        for ri in range(N_RPB_ROWS):
            acc = masked
            for ci in range(N_RPB_COLS):
                acc = jnp.where(cidx == ci, rpb_ref[base + ri * N_RPB_COLS + ci] * LOG2E, acc)
            toep_ref[ri] = acc
        for rc, step in enumerate(ROW_CLASS_STEPS):
            _, ridx0, r0 = _row_class_geometry(step % n_i, rows)
            for ar in range(BAND_ROWS):
                slab = masked
                for a in range(Q_ROWS):
                    if 0 <= ar - r0[a] < WIN_ROWS:
                        slab = jnp.where(q_row == a, toep_ref[ar - a + ridx0], slab)
                o_ref[rc, 0, 0, ar * K_COL_BLOCK:(ar + 1) * K_COL_BLOCK, e * LANES:(e + 1) * LANES] = slab


def _attn_bias_table(rpb, rows):
    assert rows // Q_ROWS >= 3 and rows % Q_ROWS == 0
    assert Q_ROWS * Q_COL_BLOCK == LANES
    n_q, n_k = HEAD_PAIR * Q_ROWS * Q_COL_BLOCK, BAND_ROWS * K_COL_BLOCK
    n_rc = len(ROW_CLASS_STEPS)
    return pl.pallas_call(
        functools.partial(_bias_kernel, rows=rows),
        grid=(N_HEAD_PAIRS, N_COL_BLOCKS),
        in_specs=[pl.BlockSpec(memory_space=pltpu.SMEM)],
        out_specs=pl.BlockSpec((n_rc, 1, 1, n_k, n_q), lambda hp, n: (0, hp, n, 0, 0)),
        out_shape=jax.ShapeDtypeStruct((n_rc, N_HEAD_PAIRS, N_COL_BLOCKS, n_k, n_q), jnp.float32),
        scratch_shapes=[pltpu.VMEM((N_RPB_ROWS, K_COL_BLOCK, LANES), jnp.float32)],
        name="bias",
    )(rpb.reshape(-1))


def _merge_kernel(x_ref, mod_ref, gpre_ref, gpost_ref, attn_ref, p_ref, pprev_ref, pnext_ref,
                  wg_ref, bg_ref, pw_ref, ps_ref, wa_ref, wb_ref, wo_ref, o_ref, ext_ref, *, seq):
    t = pl.program_id(1)
    tile = x_ref.shape[1]
    mod = mod_ref[0]
    x = x_ref[0]
    n_chunks = D_MODEL // GATE_CHUNK
    ya = _dot(attn_ref[0], wa_ref[...])
    h = _bf16(_norm_mod(x, gpre_ref[...], mod[:, D_MODEL:2 * D_MODEL], mod[:, :D_MODEL]))

    def gate_pair(c):
        lo = c * GATE_CHUNK
        g_a = _sigmoid(_dot(h, wg_ref[:, lo:lo + GATE_CHUNK]) + bg_ref[:, lo:lo + GATE_CHUNK])
        lo += D_MODEL
        g_b = _sigmoid(_dot(h, wg_ref[:, lo:lo + GATE_CHUNK]) + bg_ref[:, lo:lo + GATE_CHUNK])
        return g_a, g_b

    lead = min(3, n_chunks)
    gates = [gate_pair(c) for c in range(lead)]

    halo = POOL_HALO
    first = t == 0
    last = t == pl.num_programs(1) - 1
    ext_ref[0:halo, :] = jnp.where(first, 0.0, pprev_ref[0, 0])
    ext_ref[halo:halo + tile, :] = p_ref[0]
    ext_ref[halo + tile:2 * halo + tile, :] = jnp.where(last, 0.0, pnext_ref[0, 0])
    n_ext = tile + 2 * halo
    edge_rows = jax.lax.broadcasted_iota(jnp.int32, (halo, POOL_GROUP), 0)
    tok_head = t * tile + edge_rows
    tok_tail = t * tile + tile - halo + edge_rows
    pooled = []
    for g, w in enumerate(POOL_WINDOWS):
        sl = slice(g * POOL_GROUP, (g + 1) * POOL_GROUP)
        e = ext_ref[:, sl]
        acc = e + pltpu.roll(e, 1, 0)
        half = 1
        while 2 * half < w:
            acc = pltpu.roll(acc, half, 0) + pltpu.roll(acc, n_ext - half, 0)
            half *= 2

        def window_count(tok):
            return (jnp.minimum(tok + (w - w // 2), seq) - jnp.maximum(tok - w // 2, 0)).astype(jnp.float32)

        mean = jnp.concatenate([acc[halo:2 * halo] / window_count(tok_head),
                                acc[2 * halo:tile] * (1.0 / w),
                                acc[tile:halo + tile] / window_count(tok_tail)], axis=0)
        pooled.append(mean - e[halo:halo + tile])
    mixed = [_dot(_bf16(pooled[g]), pw_ref[g]) for g in range(N_POOL_GROUPS)]
    pool = jnp.concatenate(mixed, axis=-1) * ps_ref[...]
    yb = _dot(_bf16(pool), wb_ref[...])

    y = None
    for c in range(n_chunks):
        if c + lead < n_chunks:
            gates.append(gate_pair(c + lead))
        g_a, g_b = gates[c]
        sl = slice(c * GATE_CHUNK, (c + 1) * GATE_CHUNK)
        part = _dot(_bf16(g_a * ya[:, sl] + g_b * yb[:, sl]), wo_ref[sl, :])
        y = part if y is None else y + part
    o_ref[0] = x + _rms(y, gpost_ref[...] * mod[:, 2 * D_MODEL:3 * D_MODEL])


def _merge_call(x, mod3, g_pre, g_post, attn, p, w_g, b_gate, pool_w, pool_scale, w_a, w_b, w_o, tile):
    b, s, d = x.shape
    halo = POOL_HALO
    p4 = p.reshape(b, s // halo, halo, POOL_WIDTH)
    n_halo = s // halo
    per = tile // halo

    def const(shape):
        return pl.BlockSpec(shape, lambda i, t: (0,) * len(shape))

    return pl.pallas_call(
        functools.partial(_merge_kernel, seq=s),
        grid=(b, s // tile),
        in_specs=[pl.BlockSpec((1, tile, d), lambda i, t: (i, t, 0)),
                  pl.BlockSpec((1, 1, N_MOD * d), lambda i, t: (i, 0, 0)),
                  const((1, d)), const((1, d)),
                  pl.BlockSpec((1, tile, ATTN_WIDTH), lambda i, t: (i, t, 0)),
                  pl.BlockSpec((1, tile, POOL_WIDTH), lambda i, t: (i, t, 0)),
                  pl.BlockSpec((1, 1, halo, POOL_WIDTH), lambda i, t: (i, jnp.maximum(t * per - 1, 0), 0, 0)),
                  pl.BlockSpec((1, 1, halo, POOL_WIDTH),
                               lambda i, t: (i, jnp.minimum((t + 1) * per, n_halo - 1), 0, 0)),
                  const((d, GATE_WIDTH)), const((1, GATE_WIDTH)),
                  const((N_POOL_GROUPS, POOL_GROUP, POOL_GROUP)), const((1, POOL_WIDTH)),
                  const((ATTN_WIDTH, d)), const((POOL_WIDTH, d)), const((d, d))],
        out_specs=pl.BlockSpec((1, tile, d), lambda i, t: (i, t, 0)),
        out_shape=jax.ShapeDtypeStruct((b, s, d), jnp.float32),
        scratch_shapes=[pltpu.VMEM((tile + 2 * halo, POOL_WIDTH), jnp.float32)],
        compiler_params=pltpu.CompilerParams(vmem_limit_bytes=VMEM_LIMIT),
        name="merge",
    )(x, mod3, g_pre, g_post, attn, p, p4, p4, w_g, b_gate, pool_w, pool_scale, w_a, w_b, w_o)


def _ffn_kernel(x_ref, mod_ref, gpre_ref, gpost_ref, wu_ref, wd_ref, o_ref, *, d_ff, chunk):
    mod = mod_ref[0]
    x = x_ref[0]
    h = _bf16(_norm_mod(x, gpre_ref[...], mod[:, 4 * D_MODEL:5 * D_MODEL], mod[:, 3 * D_MODEL:4 * D_MODEL]))
    acc = None
    for c0 in range(0, d_ff, chunk):
        gate = _dot(h, wu_ref[:, c0:c0 + chunk])
        up = _dot(h, wu_ref[:, d_ff + c0:d_ff + c0 + chunk])
        a = _bf16(gate * _sigmoid(gate) * up)
        part = _dot(a, wd_ref[c0:c0 + chunk, :])
        acc = part if acc is None else acc + part
    o_ref[0] = x + _rms(acc, gpost_ref[...] * mod[:, 5 * D_MODEL:])


def _ffn_call(x, mod3, g_pre, g_post, w_up, w_down, tile):
    b, s, d = x.shape
    d_ff = w_down.shape[0]
    chunk = 2 * LANES
    assert d_ff % chunk == 0

    def const(shape):
        return pl.BlockSpec(shape, lambda i, t: (0,) * len(shape))

    return pl.pallas_call(
        functools.partial(_ffn_kernel, d_ff=d_ff, chunk=chunk),
        grid=(b, s // tile),
        in_specs=[pl.BlockSpec((1, tile, d), lambda i, t: (i, t, 0)),
                  pl.BlockSpec((1, 1, N_MOD * d), lambda i, t: (i, 0, 0)),
                  const((1, d)), const((1, d)),
                  const((d, 2 * d_ff)), const((d_ff, d))],
        out_specs=pl.BlockSpec((1, tile, d), lambda i, t: (i, t, 0)),
        out_shape=jax.ShapeDtypeStruct((b, s, d), jnp.float32),
        compiler_params=pltpu.CompilerParams(vmem_limit_bytes=VMEM_LIMIT),
        name="ffn",
    )(x, mod3, g_pre, g_post, w_up, w_down)


def _rope_tables(n_tok):
    t = jnp.arange(n_tok)
    pos = jnp.stack([t // GRID_W, t % GRID_W], axis=-1).astype(jnp.float32)
    inv = ROPE_THETA ** (-jnp.arange(ROPE_FREQS, dtype=jnp.float32) / ROPE_FREQS)
    ang = pos[:, :, None] * inv
    cos, sin = jnp.cos(ang), jnp.sin(ang)
    cos_h = jnp.concatenate([cos[:, 0], cos[:, 0], cos[:, 1], cos[:, 1]], axis=-1)
    sin_h = jnp.concatenate([-sin[:, 0], sin[:, 0], -sin[:, 1], sin[:, 1]], axis=-1)
    return jnp.tile(cos_h, (1, HEAD_PAIR)), jnp.tile(sin_h, (1, HEAD_PAIR))


def kernel(x, c, ctx, c_ctx, w_ada, b_ada, g_pre_mix, g_post_mix, g_pre_ffn, g_post_ffn, w_in, b_gate, rpb,
           pool_w, pool_scale, w_proj_a, w_proj_b, w_out, w_up, w_down):
    b, s, d = x.shape
    depth = w_ada.shape[0]
    assert depth == 1, "context stream updates are only needed for depth > 1"
    rows = s // GRID_W
    a = ATTN_WIDTH

    c8 = jnp.zeros((8, d), jnp.float32).at[:b].set(c).at[b].set(c_ctx)
    mod = _mod_call(c8, w_ada[0], b_ada[0])
    mod3 = mod[:b].reshape(b, 1, N_MOD * d)
    mod_ctx = mod[b:b + 1, :2 * d].reshape(1, 1, 2 * d)

    w = w_in[0]
    w_q = w[:, :a] * (HEAD_DIM ** -0.5 * LOG2E)
    w_qkvp = _bf16(jnp.concatenate([w_q, w[:, a:3 * a + POOL_WIDTH]], axis=1))
    w_kv_ctx = _bf16(w[:, a:3 * a])
    w_g = _bf16(w[:, 3 * a + POOL_WIDTH:])

    g_pre = g_pre_mix[0].reshape(1, d)
    k_c, v_c = _ctx_kv_call(ctx, mod_ctx, g_pre, w_kv_ctx)
    cos_t, sin_t = _rope_tables(s)
    q, k, v, p = _proj_call(x, mod3, g_pre, w_qkvp, cos_t, sin_t, tile=512)
    bias = _attn_bias_table(rpb[0], rows)
    attn = _attn_call(q, k, v, k_c, v_c, bias).reshape(b, s, a)
    x1 = _merge_call(x, mod3, g_pre, g_post_mix[0].reshape(1, d), attn, p, w_g, b_gate[0].reshape(1, -1),
                     _bf16(pool_w[0]), pool_scale[0].reshape(1, -1), _bf16(w_proj_a[0]), _bf16(w_proj_b[0]),
                     _bf16(w_out[0]), tile=512)
    return _ffn_call(x1, mod3, g_pre_ffn[0].reshape(1, d), g_post_ffn[0].reshape(1, d),
                     _bf16(w_up[0]), _bf16(w_down[0]), tile=512)
```

```python
import functools

import jax
import jax.numpy as jnp
from jax.experimental import pallas as pl
from jax.experimental.pallas import tpu as pltpu

D_MODEL = 1024
GRID_W = 64
N_HEADS = 8
HEAD_DIM = 64
ATTN_WIDTH = N_HEADS * HEAD_DIM
POOL_WINDOWS = (2, 4, 8, 16)
N_POOL_GROUPS = len(POOL_WINDOWS)
POOL_WIDTH = 512
POOL_GROUP = POOL_WIDTH // N_POOL_GROUPS
GATE_WIDTH = 2 * D_MODEL
WIN_ROWS = 8
WIN_COLS = 16
Q_COL_BLOCK = 16
K_COL_BLOCK = Q_COL_BLOCK + WIN_COLS
ROPE_FREQS = HEAD_DIM // 4
ROPE_THETA = 10000.0
N_MOD = 6
EPS = 1e-6
NEG_INF = -1e30

LANES = 128
BF16_SUBLANES = 16
LOG2E = 1.4426950408889634
HEAD_PAIR = LANES // HEAD_DIM
N_HEAD_PAIRS = N_HEADS // HEAD_PAIR
POOL_HALO = max(POOL_WINDOWS) // 2
Q_ROWS = 8
BAND_ROWS = 16
KEY_CHUNK = 256
VAL_CHUNK = 256
GATE_CHUNK = 256
PROJ_TILE = 1024
PROJ_SUB_ROWS = 512
SUB_ROWS = 512
MERGE_LAG = 4
VALUE_LAG = 2
N_COL_BLOCKS = GRID_W // Q_COL_BLOCK
VMEM_LIMIT = 56 * 1024 * 1024


def _bf16(a):
    return a.astype(jnp.bfloat16)


def _dot(a, b):
    return jnp.dot(a, b, preferred_element_type=jnp.float32)


def _rms(xf, g):
    return (xf * jax.lax.rsqrt(jnp.mean(xf * xf, axis=-1, keepdims=True) + EPS)) * g


def _sigmoid(z):
    return 1.0 / (1.0 + jnp.exp2(z * -LOG2E))


def _mod_kernel(c_ref, w_ref, b_ref, o_ref):
    c = c_ref[...]
    s = c * _sigmoid(c)
    o_ref[...] = _dot(_bf16(s), _bf16(w_ref[...])) + b_ref[...]


def _mod_call(c8, w_ada, b_ada):
    n = w_ada.shape[1]
    bn = D_MODEL
    return pl.pallas_call(
        _mod_kernel,
        grid=(n // bn,),
        in_specs=[pl.BlockSpec((8, D_MODEL), lambda j: (0, 0)),
                  pl.BlockSpec((D_MODEL, bn), lambda j: (0, j)),
                  pl.BlockSpec((1, bn), lambda j: (0, j))],
        out_specs=pl.BlockSpec((8, bn), lambda j: (0, j)),
        out_shape=jax.ShapeDtypeStruct((8, n), jnp.float32),
        name="mod",
    )(c8, w_ada, b_ada.reshape(1, n))


def _ctx_kv_kernel(ctx_ref, mod_ref, g_ref, w_ref, k_ref, v_ref):
    mod = mod_ref[0]
    gain = g_ref[...] * (1.0 + mod[:, D_MODEL:2 * D_MODEL])
    kv = _dot(_bf16(_rms(ctx_ref[0], gain) + mod[:, :D_MODEL]), w_ref[...])
    k_ref[0] = _bf16(kv[:, :ATTN_WIDTH])
    v_ref[0] = _bf16(kv[:, ATTN_WIDTH:])


def _ctx_kv_call(ctx, mod_ctx, g_pre, w_kv):
    b, l, d = ctx.shape
    out = jax.ShapeDtypeStruct((b, l, ATTN_WIDTH), jnp.bfloat16)
    return pl.pallas_call(
        _ctx_kv_kernel,
        grid=(b,),
        in_specs=[pl.BlockSpec((1, l, d), lambda i: (i, 0, 0)),
                  pl.BlockSpec((1, 1, 2 * d), lambda i: (0, 0, 0)),
                  pl.BlockSpec((1, d), lambda i: (0, 0)),
                  pl.BlockSpec((d, 2 * ATTN_WIDTH), lambda i: (0, 0))],
        out_specs=[pl.BlockSpec((1, l, ATTN_WIDTH), lambda i: (i, 0, 0))] * 2,
        out_shape=[out, out],
        name="ctx_kv",
    )(ctx, mod_ctx, g_pre, w_kv)


def _rope(a, cos, sin_signed, first_half):
    swapped = jnp.where(first_half, pltpu.roll(a, LANES - ROPE_FREQS, 1), pltpu.roll(a, ROPE_FREQS, 1))
    return a * cos + swapped * sin_signed


def _window_means_minus_token(ext, tok0, seq):
    halo = POOL_HALO
    n_ext = ext.shape[0]
    n_tok = n_ext - 2 * halo
    tok_head = tok0 + jax.lax.broadcasted_iota(jnp.int32, (halo, POOL_GROUP), 0)
    tok_tail = tok_head + (n_tok - halo)
    pooled = []
    for g, w in enumerate(POOL_WINDOWS):
        e = ext[:, g * POOL_GROUP:(g + 1) * POOL_GROUP]
        acc = e + pltpu.roll(e, 1, 0)
        half = 1
        while 2 * half < w:
            acc = pltpu.roll(acc, half, 0) + pltpu.roll(acc, n_ext - half, 0)
            half *= 2

        def window_count(tok):
            return (jnp.minimum(tok + (w - w // 2), seq) - jnp.maximum(tok - w // 2, 0)).astype(jnp.float32)

        mean = jnp.concatenate([acc[halo:2 * halo] / window_count(tok_head),
                                acc[2 * halo:n_tok] * (1.0 / w),
                                acc[n_tok:halo + n_tok] / window_count(tok_tail)], axis=0)
        pooled.append(mean - e[halo:halo + n_tok])
    return jnp.concatenate(pooled, axis=-1)


def _proj_kernel(x_ref, xprev_ref, xnext_ref, mod_ref, g_ref, w_ref, cos_ref, sin_ref,
                 q_ref, k_ref, v_ref, p_ref, *, seq):
    t = pl.program_id(1)
    tile = x_ref.shape[1]
    n_sub = tile // PROJ_SUB_ROWS
    halo = POOL_HALO
    mod = mod_ref[0]
    gain = g_ref[...] * (1.0 + mod[:, D_MODEL:2 * D_MODEL])
    shift = mod[:, :D_MODEL]
    lane = jax.lax.broadcasted_iota(jnp.int32, (PROJ_SUB_ROWS, LANES), 1)
    first_half = (lane % (2 * ROPE_FREQS)) < ROPE_FREQS
    hs = {}

    def pool_part(k):
        r = k * PROJ_SUB_ROWS
        hs[k] = _bf16(_rms(x_ref[0, r:r + PROJ_SUB_ROWS, :], gain) + shift)
        x_prev = xprev_ref[0, 0] if k == 0 else x_ref[0, r - halo:r, :]
        x_next = xnext_ref[0, 0] if k == n_sub - 1 else x_ref[0, r + PROJ_SUB_ROWS:r + PROJ_SUB_ROWS + halo, :]
        h_halo = _bf16(_rms(jnp.concatenate([x_prev, x_next], axis=0), gain) + shift)
        p_all = _dot(jnp.concatenate([hs[k], h_halo], axis=0), w_ref[:, 3 * ATTN_WIDTH:])
        p_prev = p_all[PROJ_SUB_ROWS:PROJ_SUB_ROWS + halo]
        p_next = p_all[PROJ_SUB_ROWS + halo:]
        if k == 0:
            p_prev = jnp.where(t == 0, 0.0, p_prev)
        if k == n_sub - 1:
            p_next = jnp.where(t == pl.num_programs(1) - 1, 0.0, p_next)
        ext = jnp.concatenate([p_prev, p_all[:PROJ_SUB_ROWS], p_next], axis=0)
        p_ref[0, r:r + PROJ_SUB_ROWS, :] = _bf16(_window_means_minus_token(ext, t * tile + r, seq))

    def qkv_part(k):
        r = k * PROJ_SUB_ROWS
        rows = slice(r, r + PROJ_SUB_ROWS)
        proj = _dot(hs.pop(k), w_ref[:, :3 * ATTN_WIDTH])
        cos = cos_ref[rows, :]
        sin = sin_ref[rows, :]
        for j in range(N_HEAD_PAIRS):
            sl = slice(j * LANES, (j + 1) * LANES)
            q_ref[0, rows, sl] = _bf16(_rope(proj[:, sl], cos, sin, first_half))
            ksl = slice(ATTN_WIDTH + j * LANES, ATTN_WIDTH + (j + 1) * LANES)
            k_ref[0, rows, sl] = _bf16(_rope(proj[:, ksl], cos, sin, first_half))
        v_ref[0, rows, :] = _bf16(proj[:, 2 * ATTN_WIDTH:3 * ATTN_WIDTH])

    for k in range(n_sub + 1):
        if k < n_sub:
            pool_part(k)
        if k >= 1:
            qkv_part(k - 1)


def _proj_call(x, mod3, g_pre, w_qkvp, cos_t, sin_t, tile):
    b, s, d = x.shape
    n = w_qkvp.shape[1]
    halo = POOL_HALO
    assert POOL_WIDTH == ATTN_WIDTH and 2 * halo == BF16_SUBLANES and tile % BF16_SUBLANES == 0
    x4 = x.reshape(b, s // halo, halo, d)
    n_halo = s // halo
    per = tile // halo
    act = jax.ShapeDtypeStruct((b, s, ATTN_WIDTH), jnp.bfloat16)
    act_spec = pl.BlockSpec((1, tile, ATTN_WIDTH), lambda i, t: (i, t, 0))
    return pl.pallas_call(
        functools.partial(_proj_kernel, seq=s),
        grid=(b, s // tile),
        in_specs=[pl.BlockSpec((1, tile, d), lambda i, t: (i, t, 0)),
                  pl.BlockSpec((1, 1, halo, d), lambda i, t: (i, jnp.maximum(t * per - 1, 0), 0, 0)),
                  pl.BlockSpec((1, 1, halo, d), lambda i, t: (i, jnp.minimum((t + 1) * per, n_halo - 1), 0, 0)),
                  pl.BlockSpec((1, 1, N_MOD * d), lambda i, t: (i, 0, 0)),
                  pl.BlockSpec((1, d), lambda i, t: (0, 0)),
                  pl.BlockSpec((d, n), lambda i, t: (0, 0)),
                  pl.BlockSpec((tile, LANES), lambda i, t: (t, 0)),
                  pl.BlockSpec((tile, LANES), lambda i, t: (t, 0))],
        out_specs=[act_spec] * 4,
        out_shape=[act] * 4,
        compiler_params=pltpu.CompilerParams(vmem_limit_bytes=VMEM_LIMIT),
        name="proj",
    )(x, x4, x4, mod3, g_pre, w_qkvp, cos_t, sin_t)


def _band_start(i, rows):
    return jnp.clip(i * Q_ROWS - WIN_ROWS // 2, 0, rows - BAND_ROWS)


def _key_col0(n):
    return min(max(n * Q_COL_BLOCK - WIN_COLS // 2, 0), GRID_W - K_COL_BLOCK)


def _attn_kernel(q_ref, k_ref, v_ref, kc_ref, vc_ref, bias_ref, o_ref, *, rows):
    i = pl.program_id(1)
    start = pl.multiple_of(_band_start(i, rows) * GRID_W, GRID_W)
    n_q = Q_ROWS * Q_COL_BLOCK
    n_k = BAND_ROWS * K_COL_BLOCK
    n_l = HEAD_PAIR * n_q
    n_ctx = kc_ref.shape[1]
    first_d = jax.lax.broadcasted_iota(jnp.int32, (LANES, n_q), 0) < HEAD_DIM
    zero = jnp.zeros((), jnp.bfloat16)
    ones_rows = jnp.ones((BF16_SUBLANES, VAL_CHUNK), jnp.bfloat16)
    per_val = VAL_CHUNK // KEY_CHUNK
    n_key_chunks = (n_k + n_ctx) // KEY_CHUNK
    units = [(hp, n) for hp in range(N_HEAD_PAIRS) for n in range(N_COL_BLOCKS)]
    state = [dict() for _ in units]

    def band_tile(ref, hp, n):
        c0 = _key_col0(n)
        band = ref[0, pl.ds(start, BAND_ROWS * GRID_W), hp * LANES:(hp + 1) * LANES]
        return band.reshape(BAND_ROWS, GRID_W, LANES)[:, c0:c0 + K_COL_BLOCK, :].reshape(n_k, LANES)

    def score_chunk(u, j):
        hp, n = units[u]
        st = state[u]
        if j == 0:
            q_t = q_ref[0, :, hp * LANES:(hp + 1) * LANES].reshape(Q_ROWS, GRID_W, LANES)
            q_t = q_t[:, n * Q_COL_BLOCK:(n + 1) * Q_COL_BLOCK, :].reshape(n_q, LANES).T
            st["q"] = jnp.concatenate([jnp.where(first_d, q_t, zero), jnp.where(first_d, zero, q_t)], axis=1)
            st["k"] = band_tile(k_ref, hp, n)
            st["s"] = []
        lo = j * KEY_CHUNK
        if lo < n_k:
            s_j = _dot(st["k"][lo:lo + KEY_CHUNK], st["q"]) + bias_ref[0, hp, n, lo:lo + KEY_CHUNK, :]
        else:
            s_j = _dot(kc_ref[0, lo - n_k:lo - n_k + KEY_CHUNK, hp * LANES:(hp + 1) * LANES], st["q"])
        st["s"].append(s_j)
        m_j = jnp.max(s_j.reshape(KEY_CHUNK // 8, 8, n_l), axis=0)
        st["m"] = m_j if j == 0 else jnp.maximum(st["m"], m_j)

    def value_chunk(u, jv):
        hp, n = units[u]
        st = state[u]
        if jv == 0:
            st["m"] = jnp.max(st["m"], axis=0, keepdims=True)
            st["v"] = band_tile(v_ref, hp, n)
        p_v = [_bf16(jnp.exp2(st["s"][j] - st["m"])) for j in range(jv * per_val, (jv + 1) * per_val)]
        lo = jv * VAL_CHUNK
        if lo < n_k:
            v_j = st["v"][lo:lo + VAL_CHUNK]
        else:
            v_j = vc_ref[0, lo - n_k:lo - n_k + VAL_CHUNK, hp * LANES:(hp + 1) * LANES]
        v_ext = jnp.concatenate([v_j.T, ones_rows], axis=0)
        o_j = _dot(v_ext, jnp.concatenate(p_v, axis=0))
        st["o"] = o_j if jv == 0 else st["o"] + o_j

    def finish(u):
        hp, n = units[u]
        st = state[u]
        o_t = st["o"]
        inv_l = 1.0 / o_t[LANES:LANES + 1, :]
        o_t = jnp.concatenate([o_t[:HEAD_DIM, :n_q] * inv_l[:, :n_q],
                               o_t[HEAD_DIM:LANES, n_q:] * inv_l[:, n_q:]], axis=0)
        o_ref[0, :, n * Q_COL_BLOCK:(n + 1) * Q_COL_BLOCK, hp * LANES:(hp + 1) * LANES] = (
            _bf16(o_t.T).reshape(Q_ROWS, Q_COL_BLOCK, LANES))
        st.clear()

    n_val_chunks = n_key_chunks // per_val
    n_slots = len(units) * n_val_chunks
    for slot in range(n_slots + n_val_chunks + VALUE_LAG):
        if slot < n_slots:
            u, jv = divmod(slot, n_val_chunks)
            for j in range(jv * per_val, (jv + 1) * per_val):
                score_chunk(u, j)
        done = slot - n_val_chunks - VALUE_LAG
        if done >= 0:
            u, jv = divmod(done, n_val_chunks)
            value_chunk(u, jv)
            if jv == n_val_chunks - 1:
                finish(u)


def _attn_call(q, k, v, k_c, v_c, bias):
    b, s, _ = q.shape
    rows = s // GRID_W
    n_i = rows // Q_ROWS
    l = k_c.shape[1]
    tok = Q_ROWS * GRID_W

    def row_class(i):
        return jnp.where(i == 0, 0, jnp.where(i == n_i - 1, 2, 1))

    assert (BAND_ROWS * K_COL_BLOCK) % VAL_CHUNK == 0 and l % VAL_CHUNK == 0 and VAL_CHUNK % KEY_CHUNK == 0
    whole_seq = pl.Buffered(1)
    return pl.pallas_call(
        functools.partial(_attn_kernel, rows=rows),
        grid=(b, n_i),
        in_specs=[pl.BlockSpec((1, tok, ATTN_WIDTH), lambda bi, i: (bi, i, 0)),
                  pl.BlockSpec((1, s, ATTN_WIDTH), lambda bi, i: (bi, 0, 0), pipeline_mode=whole_seq),
                  pl.BlockSpec((1, s, ATTN_WIDTH), lambda bi, i: (bi, 0, 0), pipeline_mode=whole_seq),
                  pl.BlockSpec((1, l, ATTN_WIDTH), lambda bi, i: (bi, 0, 0)),
                  pl.BlockSpec((1, l, ATTN_WIDTH), lambda bi, i: (bi, 0, 0)),
                  pl.BlockSpec((1, N_HEAD_PAIRS, N_COL_BLOCKS, BAND_ROWS * K_COL_BLOCK,
                                HEAD_PAIR * Q_ROWS * Q_COL_BLOCK),
                               lambda bi, i: (row_class(i), 0, 0, 0, 0))],
        out_specs=pl.BlockSpec((1, Q_ROWS, GRID_W, ATTN_WIDTH), lambda bi, i: (bi, i, 0, 0)),
        out_shape=jax.ShapeDtypeStruct((b, rows, GRID_W, ATTN_WIDTH), jnp.bfloat16),
        compiler_params=pltpu.CompilerParams(vmem_limit_bytes=VMEM_LIMIT),
        name="attn",
    )(q, k, v, k_c, v_c, bias)


N_RPB_ROWS = 2 * WIN_ROWS - 1
N_RPB_COLS = 2 * WIN_COLS - 1
ROW_CLASS_STEPS = (0, 1, -1)


def _row_class_geometry(i, rows):
    start = min(max(i * Q_ROWS - WIN_ROWS // 2, 0), rows - BAND_ROWS)
    r0 = [min(max(i * Q_ROWS + a - WIN_ROWS // 2, 0), rows - WIN_ROWS) - start for a in range(Q_ROWS)]
    return start, start - i * Q_ROWS + WIN_ROWS - 1, r0


def _bias_kernel(rpb_ref, o_ref, toep_ref, *, rows):
    hp = pl.program_id(0)
    n = pl.program_id(1)
    c0 = jnp.clip(n * Q_COL_BLOCK - WIN_COLS // 2, 0, GRID_W - K_COL_BLOCK)
    kc = c0 + jax.lax.broadcasted_iota(jnp.int32, (K_COL_BLOCK, LANES), 0)
    lane = jax.lax.broadcasted_iota(jnp.int32, (K_COL_BLOCK, LANES), 1)
    q_row = lane // Q_COL_BLOCK
    qc = n * Q_COL_BLOCK + lane % Q_COL_BLOCK
    q_col0 = jnp.clip(qc - WIN_COLS // 2, 0, GRID_W - WIN_COLS)
    col_valid = (kc >= q_col0) & (kc < q_col0 + WIN_COLS)
    cidx = jnp.where(col_valid, kc - qc + WIN_COLS - 1, -1)
    masked = jnp.full((K_COL_BLOCK, LANES), NEG_INF, jnp.float32)
    n_i = rows // Q_ROWS
    for e in range(HEAD_PAIR):
        base = (hp * HEAD_PAIR + e) * (N_RPB_ROWS * N_RPB_COLS)
        for ri in range(N_RPB_ROWS):
            acc = masked
            for ci in range(N_RPB_COLS):
                acc = jnp.where(cidx == ci, rpb_ref[base + ri * N_RPB_COLS + ci] * LOG2E, acc)
            toep_ref[ri] = acc
        for rc, step in enumerate(ROW_CLASS_STEPS):
            _, ridx0, r0 = _row_class_geometry(step % n_i, rows)
            for ar in range(BAND_ROWS):
                slab = masked
                for a in range(Q_ROWS):
                    if 0 <= ar - r0[a] < WIN_ROWS:
                        slab = jnp.where(q_row == a, toep_ref[ar - a + ridx0], slab)
                o_ref[rc, 0, 0, ar * K_COL_BLOCK:(ar + 1) * K_COL_BLOCK, e * LANES:(e + 1) * LANES] = slab


def _attn_bias_table(rpb, rows):
    assert rows // Q_ROWS >= 3 and rows % Q_ROWS == 0
    assert Q_ROWS * Q_COL_BLOCK == LANES
    n_q, n_k = HEAD_PAIR * Q_ROWS * Q_COL_BLOCK, BAND_ROWS * K_COL_BLOCK
    n_rc = len(ROW_CLASS_STEPS)
    return pl.pallas_call(
        functools.partial(_bias_kernel, rows=rows),
        grid=(N_HEAD_PAIRS, N_COL_BLOCKS),
        in_specs=[pl.BlockSpec(memory_space=pltpu.SMEM)],
        out_specs=pl.BlockSpec((n_rc, 1, 1, n_k, n_q), lambda hp, n: (0, hp, n, 0, 0)),
        out_shape=jax.ShapeDtypeStruct((n_rc, N_HEAD_PAIRS, N_COL_BLOCKS, n_k, n_q), jnp.float32),
        scratch_shapes=[pltpu.VMEM((N_RPB_ROWS, K_COL_BLOCK, LANES), jnp.float32)],
        name="bias",
    )(rpb.reshape(-1))


def _merge_kernel(x_ref, mod_ref, gpre_ref, gpost_ref, attn_ref, p_ref,
                  wg_ref, bg_ref, pw_ref, ps_ref, wa_ref, wb_ref, wo_ref, o_ref):
    n_sub = x_ref.shape[1] // SUB_ROWS
    mod = mod_ref[0]
    g_in = gpre_ref[...] * (1.0 + mod[:, D_MODEL:2 * D_MODEL])
    g_out = gpost_ref[...] * mod[:, 2 * D_MODEL:3 * D_MODEL]
    n_chunks = D_MODEL // GATE_CHUNK
    lead = min(3, n_chunks)
    state = [dict() for _ in range(n_sub)]

    def gate_pair(st, c):
        lo = c * GATE_CHUNK
        g_a = _sigmoid(_dot(st["h"], wg_ref[:, lo:lo + GATE_CHUNK]) + bg_ref[:, lo:lo + GATE_CHUNK])
        lo += D_MODEL
        g_b = _sigmoid(_dot(st["h"], wg_ref[:, lo:lo + GATE_CHUNK]) + bg_ref[:, lo:lo + GATE_CHUNK])
        st["gates"].append((g_a, g_b))

    def out_partial(st, c):
        g_a, g_b = st["gates"][c]
        sl = slice(c * GATE_CHUNK, (c + 1) * GATE_CHUNK)
        part = _dot(_bf16(g_a * st["ya"][:, sl] + g_b * st["yb"][:, sl]), wo_ref[sl, :])
        st["y"] = part if c == 0 else st["y"] + part

    def stage(k, s):
        st = state[k]
        rows = slice(k * SUB_ROWS, (k + 1) * SUB_ROWS)
        if s == 0:
            st["ya"] = _dot(attn_ref[0, rows, :], wa_ref[...])
            mixed = [_dot(p_ref[0, rows, g * POOL_GROUP:(g + 1) * POOL_GROUP], pw_ref[g])
                     for g in range(N_POOL_GROUPS)]
            st["yb"] = _dot(_bf16(jnp.concatenate(mixed, axis=-1) * ps_ref[...]), wb_ref[...])
        elif s <= lead:
            if s == 1:
                st["h"] = _bf16(_rms(x_ref[0, rows, :], g_in) + mod[:, :D_MODEL])
                st["gates"] = []
            gate_pair(st, s - 1)
        else:
            c = s - lead - 1
            if c + lead < n_chunks:
                gate_pair(st, c + lead)
            out_partial(st, c)
            if c == n_chunks - 1:
                o_ref[0, rows, :] = x_ref[0, rows, :] + _rms(st["y"], g_out)
                st.clear()

    n_stages = lead + 1 + n_chunks
    for step in range(n_stages + MERGE_LAG * (n_sub - 1)):
        for k in range(n_sub):
            s = step - k * MERGE_LAG
            if 0 <= s < n_stages:
                stage(k, s)


def _merge_call(x, mod3, g_pre, g_post, attn, pooled, w_g, b_gate, pool_w, pool_scale, w_a, w_b, w_o, tile):
    b, s, d = x.shape

    def const(shape):
        return pl.BlockSpec(shape, lambda i, t: (0,) * len(shape))

    return pl.pallas_call(
        _merge_kernel,
        grid=(b, s // tile),
        in_specs=[pl.BlockSpec((1, tile, d), lambda i, t: (i, t, 0)),
                  pl.BlockSpec((1, 1, N_MOD * d), lambda i, t: (i, 0, 0)),
                  const((1, d)), const((1, d)),
                  pl.BlockSpec((1, tile, ATTN_WIDTH), lambda i, t: (i, t, 0)),
                  pl.BlockSpec((1, tile, POOL_WIDTH), lambda i, t: (i, t, 0)),
                  const((d, GATE_WIDTH)), const((1, GATE_WIDTH)),
                  const((N_POOL_GROUPS, POOL_GROUP, POOL_GROUP)), const((1, POOL_WIDTH)),
                  const((ATTN_WIDTH, d)), const((POOL_WIDTH, d)), const((d, d))],
        out_specs=pl.BlockSpec((1, tile, d), lambda i, t: (i, t, 0)),
        out_shape=jax.ShapeDtypeStruct((b, s, d), jnp.float32),
        compiler_params=pltpu.CompilerParams(vmem_limit_bytes=VMEM_LIMIT),
        name="merge",
    )(x, mod3, g_pre, g_post, attn, pooled, w_g, b_gate, pool_w, pool_scale, w_a, w_b, w_o)


def _ffn_kernel(x_ref, mod_ref, gpre_ref, gpost_ref, wu_ref, wd_ref, o_ref, *, d_ff, chunk):
    mod = mod_ref[0]
    g_in = gpre_ref[...] * (1.0 + mod[:, 4 * D_MODEL:5 * D_MODEL])
    g_out = gpost_ref[...] * mod[:, 5 * D_MODEL:]
    subs = list(range(0, x_ref.shape[1], SUB_ROWS))
    n_chunks = d_ff // chunk
    lag = n_chunks // 2
    hs, accs = {}, {}

    def ffn_chunk(r, c):
        if c == 0:
            hs[r] = _bf16(_rms(x_ref[0, r:r + SUB_ROWS, :], g_in) + mod[:, 3 * D_MODEL:4 * D_MODEL])
        c0 = c * chunk
        gate = _dot(hs[r], wu_ref[:, c0:c0 + chunk])
        up = _dot(hs[r], wu_ref[:, d_ff + c0:d_ff + c0 + chunk])
        part = _dot(_bf16(gate * _sigmoid(gate) * up), wd_ref[c0:c0 + chunk, :])
        accs[r] = part if c == 0 else accs[r] + part
        if c == n_chunks - 1:
            o_ref[0, r:r + SUB_ROWS, :] = x_ref[0, r:r + SUB_ROWS, :] + _rms(accs[r], g_out)

    for step in range(n_chunks + lag * (len(subs) - 1)):
        for idx, r in enumerate(subs):
            c = step - idx * lag
            if 0 <= c < n_chunks:
                ffn_chunk(r, c)


def _ffn_call(x, mod3, g_pre, g_post, w_up, w_down, tile):
    b, s, d = x.shape
    d_ff = w_down.shape[0]
    chunk = 2 * LANES
    assert d_ff % chunk == 0

    def const(shape):
        return pl.BlockSpec(shape, lambda i, t: (0,) * len(shape))

    return pl.pallas_call(
        functools.partial(_ffn_kernel, d_ff=d_ff, chunk=chunk),
        grid=(b, s // tile),
        in_specs=[pl.BlockSpec((1, tile, d), lambda i, t: (i, t, 0)),
                  pl.BlockSpec((1, 1, N_MOD * d), lambda i, t: (i, 0, 0)),
                  const((1, d)), const((1, d)),
                  pl.BlockSpec((d, 2 * d_ff), lambda i, t: (0, 0), pipeline_mode=pl.Buffered(1)),
                  pl.BlockSpec((d_ff, d), lambda i, t: (0, 0), pipeline_mode=pl.Buffered(1))],
        out_specs=pl.BlockSpec((1, tile, d), lambda i, t: (i, t, 0)),
        out_shape=jax.ShapeDtypeStruct((b, s, d), jnp.float32),
        compiler_params=pltpu.CompilerParams(vmem_limit_bytes=VMEM_LIMIT),
        name="ffn",
    )(x, mod3, g_pre, g_post, w_up, w_down)


def _rope_tables(n_tok):
    t = jnp.arange(n_tok)
    pos = jnp.stack([t // GRID_W, t % GRID_W], axis=-1).astype(jnp.float32)
    inv = ROPE_THETA ** (-jnp.arange(ROPE_FREQS, dtype=jnp.float32) / ROPE_FREQS)
    ang = pos[:, :, None] * inv
    cos, sin = jnp.cos(ang), jnp.sin(ang)
    cos_h = jnp.concatenate([cos[:, 0], cos[:, 0], cos[:, 1], cos[:, 1]], axis=-1)
    sin_h = jnp.concatenate([-sin[:, 0], sin[:, 0], -sin[:, 1], sin[:, 1]], axis=-1)
    return jnp.tile(cos_h, (1, HEAD_PAIR)), jnp.tile(sin_h, (1, HEAD_PAIR))


def kernel(x, c, ctx, c_ctx, w_ada, b_ada, g_pre_mix, g_post_mix, g_pre_ffn, g_post_ffn, w_in, b_gate, rpb,
           pool_w, pool_scale, w_proj_a, w_proj_b, w_out, w_up, w_down):
    b, s, d = x.shape
    depth = w_ada.shape[0]
    assert depth == 1, "context stream updates are only needed for depth > 1"
    rows = s // GRID_W
    a = ATTN_WIDTH

    c8 = jnp.zeros((8, d), jnp.float32).at[:b].set(c).at[b].set(c_ctx)
    mod = _mod_call(c8, w_ada[0], b_ada[0])
    mod3 = mod[:b].reshape(b, 1, N_MOD * d)
    mod_ctx = mod[b:b + 1, :2 * d].reshape(1, 1, 2 * d)

    w = w_in[0]
    w_q = w[:, :a] * (HEAD_DIM ** -0.5 * LOG2E)
    w_qkvp = _bf16(jnp.concatenate([w_q, w[:, a:3 * a + POOL_WIDTH]], axis=1))
    w_kv_ctx = _bf16(w[:, a:3 * a])
    w_g = _bf16(w[:, 3 * a + POOL_WIDTH:])

    g_pre = g_pre_mix[0].reshape(1, d)
    k_c, v_c = _ctx_kv_call(ctx, mod_ctx, g_pre, w_kv_ctx)
    cos_t, sin_t = _rope_tables(s)
    q, k, v, pooled = _proj_call(x, mod3, g_pre, w_qkvp, cos_t, sin_t, tile=PROJ_TILE)
    bias = _attn_bias_table(rpb[0], rows)
    attn = _attn_call(q, k, v, k_c, v_c, bias).reshape(b, s, a)
    x1 = _merge_call(x, mod3, g_pre, g_post_mix[0].reshape(1, d), attn, pooled, w_g, b_gate[0].reshape(1, -1),
                     _bf16(pool_w[0]), pool_scale[0].reshape(1, -1), _bf16(w_proj_a[0]), _bf16(w_proj_b[0]),
                     _bf16(w_out[0]), tile=2 * SUB_ROWS)
    return _ffn_call(x1, mod3, g_pre_ffn[0].reshape(1, d), g_post_ffn[0].reshape(1, d),
                     _bf16(w_up[0]), _bf16(w_down[0]), tile=2 * SUB_ROWS)
```

```python
import functools

import jax
import jax.numpy as jnp
import numpy as np
from jax.experimental import pallas as pl
from jax.experimental.pallas import tpu as pltpu

D_MODEL = 1024
GRID_W = 64
N_HEADS = 8
HEAD_DIM = 64
ATTN_WIDTH = N_HEADS * HEAD_DIM
POOL_WINDOWS = (2, 4, 8, 16)
N_POOL_GROUPS = len(POOL_WINDOWS)
POOL_WIDTH = 512
POOL_GROUP = POOL_WIDTH // N_POOL_GROUPS
GATE_WIDTH = 2 * D_MODEL
WIN_ROWS = 8
WIN_COLS = 16
Q_COL_BLOCK = 16
K_COL_BLOCK = Q_COL_BLOCK + WIN_COLS
ROPE_FREQS = HEAD_DIM // 4
ROPE_THETA = 10000.0
N_MOD = 6
EPS = 1e-6
NEG_INF = -1e30

LANES = 128
BF16_SUBLANES = 16
LOG2E = 1.4426950408889634
HEAD_PAIR = LANES // HEAD_DIM
N_HEAD_PAIRS = N_HEADS // HEAD_PAIR
POOL_HALO = max(POOL_WINDOWS) // 2
Q_ROWS = 8
BAND_ROWS = 16
KEY_CHUNK = 256
VAL_CHUNK = 256
GATE_CHUNK = 256
PROJ_TILE = 1024
PROJ_SUB_ROWS = 512
SUB_ROWS = 512
MERGE_LAG = 4
VALUE_LAG = 2
N_COL_BLOCKS = GRID_W // Q_COL_BLOCK
VMEM_LIMIT = 56 * 1024 * 1024


def _bf16(a):
    return a.astype(jnp.bfloat16)


def _dot(a, b):
    return jnp.dot(a, b, preferred_element_type=jnp.float32)


def _rms(xf, g):
    return (xf * jax.lax.rsqrt(jnp.mean(xf * xf, axis=-1, keepdims=True) + EPS)) * g


def _sigmoid(z):
    return 1.0 / (1.0 + jnp.exp2(z * -LOG2E))


def _mod_kernel(c_ref, w_ref, b_ref, o_ref):
    c = c_ref[...]
    s = c * _sigmoid(c)
    o_ref[...] = _dot(_bf16(s), _bf16(w_ref[...])) + b_ref[...]


def _mod_call(c8, w_ada, b_ada):
    n = w_ada.shape[1]
    bn = D_MODEL
    return pl.pallas_call(
        _mod_kernel,
        grid=(n // bn,),
        in_specs=[pl.BlockSpec((8, D_MODEL), lambda j: (0, 0)),
                  pl.BlockSpec((D_MODEL, bn), lambda j: (0, j)),
                  pl.BlockSpec((1, bn), lambda j: (0, j))],
        out_specs=pl.BlockSpec((8, bn), lambda j: (0, j)),
        out_shape=jax.ShapeDtypeStruct((8, n), jnp.float32),
        name="mod",
    )(c8, w_ada, b_ada.reshape(1, n))


def _ctx_kv_kernel(ctx_ref, mod_ref, g_ref, wk_ref, wv_ref, k_ref, v_ref):
    mod = mod_ref[0]
    gain = g_ref[...] * (1.0 + mod[:, D_MODEL:2 * D_MODEL])
    h = _bf16(_rms(ctx_ref[0], gain) + mod[:, :D_MODEL])
    k_ref[0] = _bf16(_dot(h, wk_ref[...]))
    v_ref[0] = _bf16(_dot(h, wv_ref[...]))


def _ctx_kv_call(ctx, mod_ctx, g_pre, w_in):
    b, l, d = ctx.shape
    out = jax.ShapeDtypeStruct((b, l, ATTN_WIDTH), jnp.bfloat16)
    return pl.pallas_call(
        _ctx_kv_kernel,
        grid=(b,),
        in_specs=[pl.BlockSpec((1, l, d), lambda i: (i, 0, 0)),
                  pl.BlockSpec((1, 1, 2 * d), lambda i: (0, 0, 0)),
                  pl.BlockSpec((1, d), lambda i: (0, 0)),
                  pl.BlockSpec((d, ATTN_WIDTH), lambda i: (0, 1)),
                  pl.BlockSpec((d, ATTN_WIDTH), lambda i: (0, 2))],
        out_specs=[pl.BlockSpec((1, l, ATTN_WIDTH), lambda i: (i, 0, 0))] * 2,
        out_shape=[out, out],
        name="ctx_kv",
    )(ctx, mod_ctx, g_pre, w_in, w_in)


def _rope(a, cos, sin_signed, first_half):
    swapped = jnp.where(first_half, pltpu.roll(a, LANES - ROPE_FREQS, 1), pltpu.roll(a, ROPE_FREQS, 1))
    return a * cos + swapped * sin_signed


def _window_means_minus_token(ext, tok0, seq):
    halo = POOL_HALO
    n_ext = ext.shape[0]
    n_tok = n_ext - 2 * halo
    tok_head = tok0 + jax.lax.broadcasted_iota(jnp.int32, (halo, POOL_GROUP), 0)
    tok_tail = tok_head + (n_tok - halo)
    pooled = []
    for g, w in enumerate(POOL_WINDOWS):
        e = ext[:, g * POOL_GROUP:(g + 1) * POOL_GROUP]
        acc = e + pltpu.roll(e, 1, 0)
        half = 1
        while 2 * half < w:
            acc = pltpu.roll(acc, half, 0) + pltpu.roll(acc, n_ext - half, 0)
            half *= 2

        def window_count(tok):
            return (jnp.minimum(tok + (w - w // 2), seq) - jnp.maximum(tok - w // 2, 0)).astype(jnp.float32)

        mean = jnp.concatenate([acc[halo:2 * halo] / window_count(tok_head),
                                acc[2 * halo:n_tok] * (1.0 / w),
                                acc[n_tok:halo + n_tok] / window_count(tok_tail)], axis=0)
        pooled.append(mean - e[halo:halo + n_tok])
    return jnp.concatenate(pooled, axis=-1)


def _proj_kernel(x_ref, xprev_ref, xnext_ref, mod_ref, g_ref, w_ref, rope_ref,
                 q_ref, k_ref, v_ref, p_ref, *, seq):
    t = pl.program_id(1)
    tile = x_ref.shape[1]
    n_sub = tile // PROJ_SUB_ROWS
    halo = POOL_HALO
    mod = mod_ref[0]
    gain = g_ref[...] * (1.0 + mod[:, D_MODEL:2 * D_MODEL])
    shift = mod[:, :D_MODEL]
    lane = jax.lax.broadcasted_iota(jnp.int32, (PROJ_SUB_ROWS, LANES), 1)
    first_half = (lane % (2 * ROPE_FREQS)) < ROPE_FREQS
    hs = {}

    def pool_part(k):
        r = k * PROJ_SUB_ROWS
        hs[k] = _bf16(_rms(x_ref[0, r:r + PROJ_SUB_ROWS, :], gain) + shift)
        x_prev = xprev_ref[0, 0] if k == 0 else x_ref[0, r - halo:r, :]
        x_next = xnext_ref[0, 0] if k == n_sub - 1 else x_ref[0, r + PROJ_SUB_ROWS:r + PROJ_SUB_ROWS + halo, :]
        h_halo = _bf16(_rms(jnp.concatenate([x_prev, x_next], axis=0), gain) + shift)
        p_all = _dot(jnp.concatenate([hs[k], h_halo], axis=0), w_ref[:, 3 * ATTN_WIDTH:])
        p_prev = p_all[PROJ_SUB_ROWS:PROJ_SUB_ROWS + halo]
        p_next = p_all[PROJ_SUB_ROWS + halo:]
        if k == 0:
            p_prev = jnp.where(t == 0, 0.0, p_prev)
        if k == n_sub - 1:
            p_next = jnp.where(t == pl.num_programs(1) - 1, 0.0, p_next)
        ext = jnp.concatenate([p_prev, p_all[:PROJ_SUB_ROWS], p_next], axis=0)
        p_ref[0, r:r + PROJ_SUB_ROWS, :] = _bf16(_window_means_minus_token(ext, t * tile + r, seq))

    def qkv_part(k):
        r = k * PROJ_SUB_ROWS
        rows = slice(r, r + PROJ_SUB_ROWS)
        proj = _dot(hs.pop(k), w_ref[:, :3 * ATTN_WIDTH])
        cos_q, sin_q, cos_k, sin_k = (rope_ref[i, rows, :] for i in range(4))
        for j in range(N_HEAD_PAIRS):
            sl = slice(j * LANES, (j + 1) * LANES)
            q_ref[0, rows, sl] = _bf16(_rope(proj[:, sl], cos_q, sin_q, first_half))
            ksl = slice(ATTN_WIDTH + j * LANES, ATTN_WIDTH + (j + 1) * LANES)
            k_ref[0, rows, sl] = _bf16(_rope(proj[:, ksl], cos_k, sin_k, first_half))
        v_ref[0, rows, :] = _bf16(proj[:, 2 * ATTN_WIDTH:3 * ATTN_WIDTH])

    for k in range(n_sub + 1):
        if k < n_sub:
            pool_part(k)
        if k >= 1:
            qkv_part(k - 1)


def _proj_call(x, mod3, g_pre, w_in, rope, tile):
    b, s, d = x.shape
    n = 3 * ATTN_WIDTH + POOL_WIDTH
    halo = POOL_HALO
    assert POOL_WIDTH == ATTN_WIDTH and 2 * halo == BF16_SUBLANES and tile % BF16_SUBLANES == 0
    x4 = x.reshape(b, s // halo, halo, d)
    n_halo = s // halo
    per = tile // halo
    act = jax.ShapeDtypeStruct((b, s, ATTN_WIDTH), jnp.bfloat16)
    act_spec = pl.BlockSpec((1, tile, ATTN_WIDTH), lambda i, t: (i, t, 0))
    return pl.pallas_call(
        functools.partial(_proj_kernel, seq=s),
        grid=(b, s // tile),
        in_specs=[pl.BlockSpec((1, tile, d), lambda i, t: (i, t, 0)),
                  pl.BlockSpec((1, 1, halo, d), lambda i, t: (i, jnp.maximum(t * per - 1, 0), 0, 0)),
                  pl.BlockSpec((1, 1, halo, d), lambda i, t: (i, jnp.minimum((t + 1) * per, n_halo - 1), 0, 0)),
                  pl.BlockSpec((1, 1, N_MOD * d), lambda i, t: (i, 0, 0)),
                  pl.BlockSpec((1, d), lambda i, t: (0, 0)),
                  pl.BlockSpec((d, n), lambda i, t: (0, 0)),
                  pl.BlockSpec((4, tile, LANES), lambda i, t: (0, t, 0))],
        out_specs=[act_spec] * 4,
        out_shape=[act] * 4,
        compiler_params=pltpu.CompilerParams(vmem_limit_bytes=VMEM_LIMIT),
        name="proj",
    )(x, x4, x4, mod3, g_pre, w_in, rope)


def _band_start(i, rows):
    return jnp.clip(i * Q_ROWS - WIN_ROWS // 2, 0, rows - BAND_ROWS)


def _key_col0(n):
    return min(max(n * Q_COL_BLOCK - WIN_COLS // 2, 0), GRID_W - K_COL_BLOCK)


def _attn_kernel(q_ref, k_ref, v_ref, kc_ref, vc_ref, bias_ref, o_ref, *, rows):
    i = pl.program_id(1)
    start = pl.multiple_of(_band_start(i, rows) * GRID_W, GRID_W)
    n_q = Q_ROWS * Q_COL_BLOCK
    n_k = BAND_ROWS * K_COL_BLOCK
    n_l = HEAD_PAIR * n_q
    n_ctx = kc_ref.shape[1]
    first_d = jax.lax.broadcasted_iota(jnp.int32, (LANES, n_q), 0) < HEAD_DIM
    zero = jnp.zeros((), jnp.bfloat16)
    ones_rows = jnp.ones((BF16_SUBLANES, VAL_CHUNK), jnp.bfloat16)
    per_val = VAL_CHUNK // KEY_CHUNK
    n_key_chunks = (n_k + n_ctx) // KEY_CHUNK
    units = [(hp, n) for hp in range(N_HEAD_PAIRS) for n in range(N_COL_BLOCKS)]
    state = [dict() for _ in units]

    def band_tile(ref, hp, n):
        c0 = _key_col0(n)
        band = ref[0, pl.ds(start, BAND_ROWS * GRID_W), hp * LANES:(hp + 1) * LANES]
        return band.reshape(BAND_ROWS, GRID_W, LANES)[:, c0:c0 + K_COL_BLOCK, :].reshape(n_k, LANES)

    def score_chunk(u, j):
        hp, n = units[u]
        st = state[u]
        if j == 0:
            q_t = q_ref[0, :, hp * LANES:(hp + 1) * LANES].reshape(Q_ROWS, GRID_W, LANES)
            q_t = q_t[:, n * Q_COL_BLOCK:(n + 1) * Q_COL_BLOCK, :].reshape(n_q, LANES).T
            st["q"] = jnp.concatenate([jnp.where(first_d, q_t, zero), jnp.where(first_d, zero, q_t)], axis=1)
            st["k"] = band_tile(k_ref, hp, n)
            st["s"] = []
        lo = j * KEY_CHUNK
        if lo < n_k:
            s_j = _dot(st["k"][lo:lo + KEY_CHUNK], st["q"]) + bias_ref[0, hp, n, lo:lo + KEY_CHUNK, :]
        else:
            s_j = _dot(kc_ref[0, lo - n_k:lo - n_k + KEY_CHUNK, hp * LANES:(hp + 1) * LANES], st["q"])
        st["s"].append(s_j)
        m_j = jnp.max(s_j.reshape(KEY_CHUNK // 8, 8, n_l), axis=0)
        st["m"] = m_j if j == 0 else jnp.maximum(st["m"], m_j)

    def value_chunk(u, jv):
        hp, n = units[u]
        st = state[u]
        if jv == 0:
            st["m"] = jnp.max(st["m"], axis=0, keepdims=True)
            st["v"] = band_tile(v_ref, hp, n)
        p_v = [_bf16(jnp.exp2(st["s"][j] - st["m"])) for j in range(jv * per_val, (jv + 1) * per_val)]
        lo = jv * VAL_CHUNK
        if lo < n_k:
            v_j = st["v"][lo:lo + VAL_CHUNK]
        else:
            v_j = vc_ref[0, lo - n_k:lo - n_k + VAL_CHUNK, hp * LANES:(hp + 1) * LANES]
        v_ext = jnp.concatenate([v_j.T, ones_rows], axis=0)
        o_j = _dot(v_ext, jnp.concatenate(p_v, axis=0))
        st["o"] = o_j if jv == 0 else st["o"] + o_j

    def finish(u):
        hp, n = units[u]
        st = state[u]
        o_t = st["o"]
        inv_l = 1.0 / o_t[LANES:LANES + 1, :]
        o_t = jnp.concatenate([o_t[:HEAD_DIM, :n_q] * inv_l[:, :n_q],
                               o_t[HEAD_DIM:LANES, n_q:] * inv_l[:, n_q:]], axis=0)
        o_ref[0, :, n * Q_COL_BLOCK:(n + 1) * Q_COL_BLOCK, hp * LANES:(hp + 1) * LANES] = (
            _bf16(o_t.T).reshape(Q_ROWS, Q_COL_BLOCK, LANES))
        st.clear()

    n_val_chunks = n_key_chunks // per_val
    n_slots = len(units) * n_val_chunks
    for slot in range(n_slots + n_val_chunks + VALUE_LAG):
        if slot < n_slots:
            u, jv = divmod(slot, n_val_chunks)
            for j in range(jv * per_val, (jv + 1) * per_val):
                score_chunk(u, j)
        done = slot - n_val_chunks - VALUE_LAG
        if done >= 0:
            u, jv = divmod(done, n_val_chunks)
            value_chunk(u, jv)
            if jv == n_val_chunks - 1:
                finish(u)


def _attn_call(q, k, v, k_c, v_c, bias):
    b, s, _ = q.shape
    rows = s // GRID_W
    n_i = rows // Q_ROWS
    l = k_c.shape[1]
    tok = Q_ROWS * GRID_W

    def row_class(i):
        return jnp.where(i == 0, 0, jnp.where(i == n_i - 1, 2, 1))

    assert (BAND_ROWS * K_COL_BLOCK) % VAL_CHUNK == 0 and l % VAL_CHUNK == 0 and VAL_CHUNK % KEY_CHUNK == 0
    whole_seq = pl.Buffered(2)
    return pl.pallas_call(
        functools.partial(_attn_kernel, rows=rows),
        grid=(b, n_i),
        in_specs=[pl.BlockSpec((1, tok, ATTN_WIDTH), lambda bi, i: (bi, i, 0)),
                  pl.BlockSpec((1, s, ATTN_WIDTH), lambda bi, i: (bi, 0, 0), pipeline_mode=whole_seq),
                  pl.BlockSpec((1, s, ATTN_WIDTH), lambda bi, i: (bi, 0, 0), pipeline_mode=whole_seq),
                  pl.BlockSpec((1, l, ATTN_WIDTH), lambda bi, i: (bi, 0, 0)),
                  pl.BlockSpec((1, l, ATTN_WIDTH), lambda bi, i: (bi, 0, 0)),
                  pl.BlockSpec((1, N_HEAD_PAIRS, N_COL_BLOCKS, BAND_ROWS * K_COL_BLOCK,
                                HEAD_PAIR * Q_ROWS * Q_COL_BLOCK),
                               lambda bi, i: (row_class(i), 0, 0, 0, 0))],
        out_specs=pl.BlockSpec((1, Q_ROWS, GRID_W, ATTN_WIDTH), lambda bi, i: (bi, i, 0, 0)),
        out_shape=jax.ShapeDtypeStruct((b, rows, GRID_W, ATTN_WIDTH), jnp.bfloat16),
        compiler_params=pltpu.CompilerParams(vmem_limit_bytes=VMEM_LIMIT),
        name="attn",
    )(q, k, v, k_c, v_c, bias)


N_RPB_ROWS = 2 * WIN_ROWS - 1
N_RPB_COLS = 2 * WIN_COLS - 1
ROW_CLASS_STEPS = (0, 1, -1)


def _row_class_geometry(i, rows):
    start = min(max(i * Q_ROWS - WIN_ROWS // 2, 0), rows - BAND_ROWS)
    r0 = [min(max(i * Q_ROWS + a - WIN_ROWS // 2, 0), rows - WIN_ROWS) - start for a in range(Q_ROWS)]
    return start, start - i * Q_ROWS + WIN_ROWS - 1, r0


def _bias_kernel(rpb_ref, o_ref, toep_ref, *, rows):
    hp = pl.program_id(0)
    n = pl.program_id(1)
    c0 = jnp.clip(n * Q_COL_BLOCK - WIN_COLS // 2, 0, GRID_W - K_COL_BLOCK)
    kc = c0 + jax.lax.broadcasted_iota(jnp.int32, (K_COL_BLOCK, LANES), 0)
    lane = jax.lax.broadcasted_iota(jnp.int32, (K_COL_BLOCK, LANES), 1)
    q_row = lane // Q_COL_BLOCK
    qc = n * Q_COL_BLOCK + lane % Q_COL_BLOCK
    q_col0 = jnp.clip(qc - WIN_COLS // 2, 0, GRID_W - WIN_COLS)
    col_valid = (kc >= q_col0) & (kc < q_col0 + WIN_COLS)
    cidx = jnp.where(col_valid, kc - qc + WIN_COLS - 1, -1)
    masked = jnp.full((K_COL_BLOCK, LANES), NEG_INF, jnp.float32)
    n_i = rows // Q_ROWS
    for e in range(HEAD_PAIR):
        base = (hp * HEAD_PAIR + e) * (N_RPB_ROWS * N_RPB_COLS)
        for ri in range(N_RPB_ROWS):
            acc = masked
            for ci in range(N_RPB_COLS):
                acc = jnp.where(cidx == ci, rpb_ref[base + ri * N_RPB_COLS + ci] * LOG2E, acc)
            toep_ref[ri] = acc
        for rc, step in enumerate(ROW_CLASS_STEPS):
            _, ridx0, r0 = _row_class_geometry(step % n_i, rows)
            for ar in range(BAND_ROWS):
                slab = masked
                for a in range(Q_ROWS):
                    if 0 <= ar - r0[a] < WIN_ROWS:
                        slab = jnp.where(q_row == a, toep_ref[ar - a + ridx0], slab)
                o_ref[rc, 0, 0, ar * K_COL_BLOCK:(ar + 1) * K_COL_BLOCK, e * LANES:(e + 1) * LANES] = slab


def _attn_bias_table(rpb, rows):
    assert rows // Q_ROWS >= 3 and rows % Q_ROWS == 0
    assert Q_ROWS * Q_COL_BLOCK == LANES
    n_q, n_k = HEAD_PAIR * Q_ROWS * Q_COL_BLOCK, BAND_ROWS * K_COL_BLOCK
    n_rc = len(ROW_CLASS_STEPS)
    return pl.pallas_call(
        functools.partial(_bias_kernel, rows=rows),
        grid=(N_HEAD_PAIRS, N_COL_BLOCKS),
        in_specs=[pl.BlockSpec(memory_space=pltpu.SMEM)],
        out_specs=pl.BlockSpec((n_rc, 1, 1, n_k, n_q), lambda hp, n: (0, hp, n, 0, 0)),
        out_shape=jax.ShapeDtypeStruct((n_rc, N_HEAD_PAIRS, N_COL_BLOCKS, n_k, n_q), jnp.float32),
        scratch_shapes=[pltpu.VMEM((N_RPB_ROWS, K_COL_BLOCK, LANES), jnp.float32)],
        name="bias",
    )(rpb.reshape(-1))


def _merge_kernel(x_ref, mod_ref, gpre_ref, gpost_ref, attn_ref, p_ref,
                  wg_ref, bg_ref, pw_ref, ps_ref, wa_ref, wb_ref, wo_ref, o_ref):
    n_sub = x_ref.shape[1] // SUB_ROWS
    mod = mod_ref[0]
    g_in = gpre_ref[...] * (1.0 + mod[:, D_MODEL:2 * D_MODEL])
    g_out = gpost_ref[...] * mod[:, 2 * D_MODEL:3 * D_MODEL]
    n_chunks = D_MODEL // GATE_CHUNK
    lead = min(3, n_chunks)
    state = [dict() for _ in range(n_sub)]

    def gate_pair(st, c):
        lo = c * GATE_CHUNK
        g_a = _sigmoid(_dot(st["h"], wg_ref[:, lo:lo + GATE_CHUNK]) + bg_ref[:, lo:lo + GATE_CHUNK])
        lo += D_MODEL
        g_b = _sigmoid(_dot(st["h"], wg_ref[:, lo:lo + GATE_CHUNK]) + bg_ref[:, lo:lo + GATE_CHUNK])
        st["gates"].append((g_a, g_b))

    def out_partial(st, c):
        g_a, g_b = st["gates"][c]
        sl = slice(c * GATE_CHUNK, (c + 1) * GATE_CHUNK)
        part = _dot(_bf16(g_a * st["ya"][:, sl] + g_b * st["yb"][:, sl]), wo_ref[sl, :])
        st["y"] = part if c == 0 else st["y"] + part

    def stage(k, s):
        st = state[k]
        rows = slice(k * SUB_ROWS, (k + 1) * SUB_ROWS)
        if s == 0:
            st["ya"] = _dot(attn_ref[0, rows, :], wa_ref[...])
            mixed = [_dot(p_ref[0, rows, g * POOL_GROUP:(g + 1) * POOL_GROUP], pw_ref[g])
                     for g in range(N_POOL_GROUPS)]
            st["yb"] = _dot(_bf16(jnp.concatenate(mixed, axis=-1) * ps_ref[...]), wb_ref[...])
        elif s <= lead:
            if s == 1:
                st["h"] = _bf16(_rms(x_ref[0, rows, :], g_in) + mod[:, :D_MODEL])
                st["gates"] = []
            gate_pair(st, s - 1)
        else:
            c = s - lead - 1
            if c + lead < n_chunks:
                gate_pair(st, c + lead)
            out_partial(st, c)
            if c == n_chunks - 1:
                o_ref[0, rows, :] = x_ref[0, rows, :] + _rms(st["y"], g_out)
                st.clear()

    n_stages = lead + 1 + n_chunks
    for step in range(n_stages + MERGE_LAG * (n_sub - 1)):
        for k in range(n_sub):
            s = step - k * MERGE_LAG
            if 0 <= s < n_stages:
                stage(k, s)


def _merge_call(x, mod3, g_pre, g_post, attn, pooled, w_in, b_gate, pool_w, pool_scale, w_a, w_b, w_o, tile):
    b, s, d = x.shape
    assert w_in.shape[1] == 2 * GATE_WIDTH

    def const(shape):
        return pl.BlockSpec(shape, lambda i, t: (0,) * len(shape))

    return pl.pallas_call(
        _merge_kernel,
        grid=(b, s // tile),
        in_specs=[pl.BlockSpec((1, tile, d), lambda i, t: (i, t, 0)),
                  pl.BlockSpec((1, 1, N_MOD * d), lambda i, t: (i, 0, 0)),
                  const((1, d)), const((1, d)),
                  pl.BlockSpec((1, tile, ATTN_WIDTH), lambda i, t: (i, t, 0)),
                  pl.BlockSpec((1, tile, POOL_WIDTH), lambda i, t: (i, t, 0)),
                  pl.BlockSpec((d, GATE_WIDTH), lambda i, t: (0, 1)), const((1, GATE_WIDTH)),
                  const((N_POOL_GROUPS, POOL_GROUP, POOL_GROUP)), const((1, POOL_WIDTH)),
                  const((ATTN_WIDTH, d)), const((POOL_WIDTH, d)), const((d, d))],
        out_specs=pl.BlockSpec((1, tile, d), lambda i, t: (i, t, 0)),
        out_shape=jax.ShapeDtypeStruct((b, s, d), jnp.float32),
        compiler_params=pltpu.CompilerParams(vmem_limit_bytes=VMEM_LIMIT),
        name="merge",
    )(x, mod3, g_pre, g_post, attn, pooled, w_in, b_gate, pool_w, pool_scale, w_a, w_b, w_o)


def _ffn_kernel(x_ref, mod_ref, gpre_ref, gpost_ref, wu_ref, wd_ref, o_ref, *, d_ff, chunk):
    mod = mod_ref[0]
    g_in = gpre_ref[...] * (1.0 + mod[:, 4 * D_MODEL:5 * D_MODEL])
    g_out = gpost_ref[...] * mod[:, 5 * D_MODEL:]
    subs = list(range(0, x_ref.shape[1], SUB_ROWS))
    n_chunks = d_ff // chunk
    lag = n_chunks // 2
    hs, accs = {}, {}

    def ffn_chunk(r, c):
        if c == 0:
            hs[r] = _bf16(_rms(x_ref[0, r:r + SUB_ROWS, :], g_in) + mod[:, 3 * D_MODEL:4 * D_MODEL])
        c0 = c * chunk
        gate = _dot(hs[r], wu_ref[:, c0:c0 + chunk])
        up = _dot(hs[r], wu_ref[:, d_ff + c0:d_ff + c0 + chunk])
        part = _dot(_bf16(gate * _sigmoid(gate) * up), wd_ref[c0:c0 + chunk, :])
        accs[r] = part if c == 0 else accs[r] + part
        if c == n_chunks - 1:
            o_ref[0, r:r + SUB_ROWS, :] = x_ref[0, r:r + SUB_ROWS, :] + _rms(accs[r], g_out)

    for step in range(n_chunks + lag * (len(subs) - 1)):
        for idx, r in enumerate(subs):
            c = step - idx * lag
            if 0 <= c < n_chunks:
                ffn_chunk(r, c)


def _ffn_call(x, mod3, g_pre, g_post, w_up, w_down, tile):
    b, s, d = x.shape
    d_ff = w_down.shape[0]
    chunk = 2 * LANES
    assert d_ff % chunk == 0

    def const(shape):
        return pl.BlockSpec(shape, lambda i, t: (0,) * len(shape))

    return pl.pallas_call(
        functools.partial(_ffn_kernel, d_ff=d_ff, chunk=chunk),
        grid=(b, s // tile),
        in_specs=[pl.BlockSpec((1, tile, d), lambda i, t: (i, t, 0)),
                  pl.BlockSpec((1, 1, N_MOD * d), lambda i, t: (i, 0, 0)),
                  const((1, d)), const((1, d)),
                  pl.BlockSpec((d, 2 * d_ff), lambda i, t: (0, 0), pipeline_mode=pl.Buffered(1)),
                  pl.BlockSpec((d_ff, d), lambda i, t: (0, 0), pipeline_mode=pl.Buffered(1))],
        out_specs=pl.BlockSpec((1, tile, d), lambda i, t: (i, t, 0)),
        out_shape=jax.ShapeDtypeStruct((b, s, d), jnp.float32),
        compiler_params=pltpu.CompilerParams(vmem_limit_bytes=VMEM_LIMIT),
        name="ffn",
    )(x, mod3, g_pre, g_post, w_up, w_down)


def _rope_tables(n_tok):
    t = np.arange(n_tok)
    pos = np.stack([t // GRID_W, t % GRID_W], axis=-1).astype(np.float64)
    inv = ROPE_THETA ** (-np.arange(ROPE_FREQS, dtype=np.float64) / ROPE_FREQS)
    ang = pos[:, :, None] * inv
    cos, sin = np.cos(ang), np.sin(ang)
    cos_h = np.concatenate([cos[:, 0], cos[:, 0], cos[:, 1], cos[:, 1]], axis=-1)
    sin_h = np.concatenate([-sin[:, 0], sin[:, 0], -sin[:, 1], sin[:, 1]], axis=-1)
    cos_p, sin_p = np.tile(cos_h, (1, HEAD_PAIR)), np.tile(sin_h, (1, HEAD_PAIR))
    q_scale = HEAD_DIM ** -0.5 * LOG2E
    return jnp.asarray(np.stack([cos_p * q_scale, sin_p * q_scale, cos_p, sin_p]).astype(np.float32))


def kernel(x, c, ctx, c_ctx, w_ada, b_ada, g_pre_mix, g_post_mix, g_pre_ffn, g_post_ffn, w_in, b_gate, rpb,
           pool_w, pool_scale, w_proj_a, w_proj_b, w_out, w_up, w_down):
    b, s, d = x.shape
    depth = w_ada.shape[0]
    assert depth == 1, "context stream updates are only needed for depth > 1"
    rows = s // GRID_W
    a = ATTN_WIDTH

    c8 = jnp.zeros((8, d), jnp.float32).at[:b].set(c).at[b].set(c_ctx)
    mod = _mod_call(c8, w_ada[0], b_ada[0])
    mod3 = mod[:b].reshape(b, 1, N_MOD * d)
    mod_ctx = mod[b:b + 1, :2 * d].reshape(1, 1, 2 * d)

    w_in_b = _bf16(w_in[0])
    g_pre = g_pre_mix[0].reshape(1, d)
    k_c, v_c = _ctx_kv_call(ctx, mod_ctx, g_pre, w_in_b)
    q, k, v, pooled = _proj_call(x, mod3, g_pre, w_in_b, _rope_tables(s), tile=PROJ_TILE)
    bias = _attn_bias_table(rpb[0], rows)
    attn = _attn_call(q, k, v, k_c, v_c, bias).reshape(b, s, a)
    x1 = _merge_call(x, mod3, g_pre, g_post_mix[0].reshape(1, d), attn, pooled, w_in_b, b_gate[0].reshape(1, -1),
                     _bf16(pool_w[0]), pool_scale[0].reshape(1, -1), _bf16(w_proj_a[0]), _bf16(w_proj_b[0]),
                     _bf16(w_out[0]), tile=2 * SUB_ROWS)
    return _ffn_call(x1, mod3, g_pre_ffn[0].reshape(1, d), g_post_ffn[0].reshape(1, d),
                     _bf16(w_up[0]), _bf16(w_down[0]), tile=2 * SUB_ROWS)
```

```python
import functools

import jax
import jax.numpy as jnp
import numpy as np
from jax.experimental import pallas as pl
from jax.experimental.pallas import tpu as pltpu

D_MODEL = 1024
GRID_W = 64
N_HEADS = 8
HEAD_DIM = 64
ATTN_WIDTH = N_HEADS * HEAD_DIM
POOL_WINDOWS = (2, 4, 8, 16)
N_POOL_GROUPS = len(POOL_WINDOWS)
POOL_WIDTH = 512
POOL_GROUP = POOL_WIDTH // N_POOL_GROUPS
GATE_WIDTH = 2 * D_MODEL
WIN_ROWS = 8
WIN_COLS = 16
Q_COL_BLOCK = 16
K_COL_BLOCK = Q_COL_BLOCK + WIN_COLS
ROPE_FREQS = HEAD_DIM // 4
ROPE_THETA = 10000.0
N_MOD = 6
EPS = 1e-6
NEG_INF = -1e30

LANES = 128
BF16_SUBLANES = 16
LOG2E = 1.4426950408889634
HEAD_PAIR = LANES // HEAD_DIM
N_HEAD_PAIRS = N_HEADS // HEAD_PAIR
POOL_HALO = max(POOL_WINDOWS) // 2
Q_ROWS = 8
BAND_ROWS = 16
KEY_CHUNK = 256
VAL_CHUNK = 256
GATE_CHUNK = 256
PROJ_TILE = 1024
PROJ_SUB_ROWS = 512
SUB_ROWS = 512
MERGE_LAG = 4
VALUE_LAG = 2
N_COL_BLOCKS = GRID_W // Q_COL_BLOCK
VMEM_LIMIT = 56 * 1024 * 1024


def _bf16(a):
    return a.astype(jnp.bfloat16)


def _dot(a, b):
    return jnp.dot(a, b, preferred_element_type=jnp.float32)


def _rms(xf, g):
    return (xf * jax.lax.rsqrt(jnp.mean(xf * xf, axis=-1, keepdims=True) + EPS)) * g


def _sigmoid(z):
    return 1.0 / (1.0 + jnp.exp2(z * -LOG2E))


def _mod_kernel(c_ref, w_ref, b_ref, o_ref):
    c = c_ref[...]
    s = c * _sigmoid(c)
    o_ref[...] = _dot(_bf16(s), _bf16(w_ref[...])) + b_ref[...]


def _mod_call(c8, w_ada, b_ada):
    n = w_ada.shape[1]
    bn = D_MODEL
    return pl.pallas_call(
        _mod_kernel,
        grid=(n // bn,),
        in_specs=[pl.BlockSpec((8, D_MODEL), lambda j: (0, 0)),
                  pl.BlockSpec((D_MODEL, bn), lambda j: (0, j)),
                  pl.BlockSpec((1, bn), lambda j: (0, j))],
        out_specs=pl.BlockSpec((8, bn), lambda j: (0, j)),
        out_shape=jax.ShapeDtypeStruct((8, n), jnp.float32),
        name="mod",
    )(c8, w_ada, b_ada.reshape(1, n))


def _ctx_kv_kernel(ctx_ref, mod_ref, g_ref, wk_ref, wv_ref, k_ref, v_ref):
    mod = mod_ref[0]
    gain = g_ref[...] * (1.0 + mod[:, D_MODEL:2 * D_MODEL])
    h = _bf16(_rms(ctx_ref[0], gain) + mod[:, :D_MODEL])
    k_ref[0] = _bf16(_dot(h, wk_ref[...]))
    v_ref[0] = _bf16(_dot(h, wv_ref[...]))


def _ctx_kv_call(ctx, mod_ctx, g_pre, w_in):
    b, l, d = ctx.shape
    out = jax.ShapeDtypeStruct((b, l, ATTN_WIDTH), jnp.bfloat16)
    return pl.pallas_call(
        _ctx_kv_kernel,
        grid=(b,),
        in_specs=[pl.BlockSpec((1, l, d), lambda i: (i, 0, 0)),
                  pl.BlockSpec((1, 1, 2 * d), lambda i: (0, 0, 0)),
                  pl.BlockSpec((1, d), lambda i: (0, 0)),
                  pl.BlockSpec((d, ATTN_WIDTH), lambda i: (0, 1)),
                  pl.BlockSpec((d, ATTN_WIDTH), lambda i: (0, 2))],
        out_specs=[pl.BlockSpec((1, l, ATTN_WIDTH), lambda i: (i, 0, 0))] * 2,
        out_shape=[out, out],
        name="ctx_kv",
    )(ctx, mod_ctx, g_pre, w_in, w_in)


def _rope(a, cos, sin_signed, first_half):
    swapped = jnp.where(first_half, pltpu.roll(a, LANES - ROPE_FREQS, 1), pltpu.roll(a, ROPE_FREQS, 1))
    return a * cos + swapped * sin_signed


def _window_means_minus_token(ext, tok0, seq):
    halo = POOL_HALO
    n_ext = ext.shape[0]
    n_tok = n_ext - 2 * halo
    tok_head = tok0 + jax.lax.broadcasted_iota(jnp.int32, (halo, POOL_GROUP), 0)
    tok_tail = tok_head + (n_tok - halo)
    pooled = []
    for g, w in enumerate(POOL_WINDOWS):
        e = ext[:, g * POOL_GROUP:(g + 1) * POOL_GROUP]
        acc = e + pltpu.roll(e, 1, 0)
        half = 1
        while 2 * half < w:
            acc = pltpu.roll(acc, half, 0) + pltpu.roll(acc, n_ext - half, 0)
            half *= 2

        def window_count(tok):
            return (jnp.minimum(tok + (w - w // 2), seq) - jnp.maximum(tok - w // 2, 0)).astype(jnp.float32)

        mean = jnp.concatenate([acc[halo:2 * halo] / window_count(tok_head),
                                acc[2 * halo:n_tok] * (1.0 / w),
                                acc[n_tok:halo + n_tok] / window_count(tok_tail)], axis=0)
        pooled.append(mean - e[halo:halo + n_tok])
    return jnp.concatenate(pooled, axis=-1)


def _proj_kernel(x_ref, xprev_ref, xnext_ref, mod_ref, g_ref, w_ref, rope_ref,
                 q_ref, k_ref, v_ref, p_ref, *, seq):
    t = pl.program_id(1)
    tile = x_ref.shape[1]
    n_sub = tile // PROJ_SUB_ROWS
    halo = POOL_HALO
    mod = mod_ref[0]
    gain = g_ref[...] * (1.0 + mod[:, D_MODEL:2 * D_MODEL])
    shift = mod[:, :D_MODEL]
    lane = jax.lax.broadcasted_iota(jnp.int32, (PROJ_SUB_ROWS, LANES), 1)
    first_half = (lane % (2 * ROPE_FREQS)) < ROPE_FREQS
    hs = {}

    def pool_part(k):
        r = k * PROJ_SUB_ROWS
        hs[k] = _bf16(_rms(x_ref[0, r:r + PROJ_SUB_ROWS, :], gain) + shift)
        x_prev = xprev_ref[0, 0] if k == 0 else x_ref[0, r - halo:r, :]
        x_next = xnext_ref[0, 0] if k == n_sub - 1 else x_ref[0, r + PROJ_SUB_ROWS:r + PROJ_SUB_ROWS + halo, :]
        h_halo = _bf16(_rms(jnp.concatenate([x_prev, x_next], axis=0), gain) + shift)
        p_all = _dot(jnp.concatenate([hs[k], h_halo], axis=0), w_ref[:, 3 * ATTN_WIDTH:])
        p_prev = p_all[PROJ_SUB_ROWS:PROJ_SUB_ROWS + halo]
        p_next = p_all[PROJ_SUB_ROWS + halo:]
        if k == 0:
            p_prev = jnp.where(t == 0, 0.0, p_prev)
        if k == n_sub - 1:
            p_next = jnp.where(t == pl.num_programs(1) - 1, 0.0, p_next)
        ext = jnp.concatenate([p_prev, p_all[:PROJ_SUB_ROWS], p_next], axis=0)
        p_ref[0, r:r + PROJ_SUB_ROWS, :] = _bf16(_window_means_minus_token(ext, t * tile + r, seq))

    def qkv_part(k):
        r = k * PROJ_SUB_ROWS
        rows = slice(r, r + PROJ_SUB_ROWS)
        proj = _dot(hs.pop(k), w_ref[:, :3 * ATTN_WIDTH])
        cos_q, sin_q, cos_k, sin_k = (rope_ref[i, rows, :] for i in range(4))
        for j in range(N_HEAD_PAIRS):
            sl = slice(j * LANES, (j + 1) * LANES)
            q_ref[0, rows, sl] = _bf16(_rope(proj[:, sl], cos_q, sin_q, first_half))
            ksl = slice(ATTN_WIDTH + j * LANES, ATTN_WIDTH + (j + 1) * LANES)
            k_ref[0, rows, sl] = _bf16(_rope(proj[:, ksl], cos_k, sin_k, first_half))
        v_ref[0, rows, :] = _bf16(proj[:, 2 * ATTN_WIDTH:3 * ATTN_WIDTH])

    for k in range(n_sub):
        pool_part(k)
        qkv_part(k)


def _proj_call(x, mod3, g_pre, w_in, rope, tile):
    b, s, d = x.shape
    n = 3 * ATTN_WIDTH + POOL_WIDTH
    halo = POOL_HALO
    assert POOL_WIDTH == ATTN_WIDTH and 2 * halo == BF16_SUBLANES and tile % BF16_SUBLANES == 0
    x4 = x.reshape(b, s // halo, halo, d)
    n_halo = s // halo
    per = tile // halo
    act = jax.ShapeDtypeStruct((b, s, ATTN_WIDTH), jnp.bfloat16)
    act_spec = pl.BlockSpec((1, tile, ATTN_WIDTH), lambda i, t: (i, t, 0))
    return pl.pallas_call(
        functools.partial(_proj_kernel, seq=s),
        grid=(b, s // tile),
        in_specs=[pl.BlockSpec((1, tile, d), lambda i, t: (i, t, 0)),
                  pl.BlockSpec((1, 1, halo, d), lambda i, t: (i, jnp.maximum(t * per - 1, 0), 0, 0)),
                  pl.BlockSpec((1, 1, halo, d), lambda i, t: (i, jnp.minimum((t + 1) * per, n_halo - 1), 0, 0)),
                  pl.BlockSpec((1, 1, N_MOD * d), lambda i, t: (i, 0, 0)),
                  pl.BlockSpec((1, d), lambda i, t: (0, 0)),
                  pl.BlockSpec((d, n), lambda i, t: (0, 0)),
                  pl.BlockSpec((4, tile, LANES), lambda i, t: (0, t, 0))],
        out_specs=[act_spec] * 4,
        out_shape=[act] * 4,
        compiler_params=pltpu.CompilerParams(vmem_limit_bytes=VMEM_LIMIT),
        name="proj",
    )(x, x4, x4, mod3, g_pre, w_in, rope)


def _band_start(i, rows):
    return jnp.clip(i * Q_ROWS - WIN_ROWS // 2, 0, rows - BAND_ROWS)


def _key_col0(n):
    return min(max(n * Q_COL_BLOCK - WIN_COLS // 2, 0), GRID_W - K_COL_BLOCK)


def _attn_kernel(q_ref, k_ref, v_ref, kc_ref, vc_ref, bias_ref, o_ref, *, rows):
    i = pl.program_id(1)
    start = pl.multiple_of(_band_start(i, rows) * GRID_W, GRID_W)
    n_q = Q_ROWS * Q_COL_BLOCK
    n_k = BAND_ROWS * K_COL_BLOCK
    n_l = HEAD_PAIR * n_q
    n_ctx = kc_ref.shape[1]
    first_d = jax.lax.broadcasted_iota(jnp.int32, (LANES, n_q), 0) < HEAD_DIM
    zero = jnp.zeros((), jnp.bfloat16)
    ones_rows = jnp.ones((BF16_SUBLANES, VAL_CHUNK), jnp.bfloat16)
    per_val = VAL_CHUNK // KEY_CHUNK
    n_key_chunks = (n_k + n_ctx) // KEY_CHUNK
    units = [(hp, n) for hp in range(N_HEAD_PAIRS) for n in range(N_COL_BLOCKS)]
    state = [dict() for _ in units]

    def band_tile(ref, hp, n):
        c0 = _key_col0(n)
        band = ref[0, pl.ds(start, BAND_ROWS * GRID_W), hp * LANES:(hp + 1) * LANES]
        return band.reshape(BAND_ROWS, GRID_W, LANES)[:, c0:c0 + K_COL_BLOCK, :].reshape(n_k, LANES)

    def score_chunk(u, j):
        hp, n = units[u]
        st = state[u]
        if j == 0:
            q_t = q_ref[0, :, hp * LANES:(hp + 1) * LANES].reshape(Q_ROWS, GRID_W, LANES)
            q_t = q_t[:, n * Q_COL_BLOCK:(n + 1) * Q_COL_BLOCK, :].reshape(n_q, LANES).T
            st["q"] = jnp.concatenate([jnp.where(first_d, q_t, zero), jnp.where(first_d, zero, q_t)], axis=1)
            st["k"] = band_tile(k_ref, hp, n)
            st["s"] = []
        lo = j * KEY_CHUNK
        if lo < n_k:
            s_j = _dot(st["k"][lo:lo + KEY_CHUNK], st["q"]) + bias_ref[0, hp, n, lo:lo + KEY_CHUNK, :]
        else:
            s_j = _dot(kc_ref[0, lo - n_k:lo - n_k + KEY_CHUNK, hp * LANES:(hp + 1) * LANES], st["q"])
        st["s"].append(s_j)
        m_j = jnp.max(s_j.reshape(KEY_CHUNK // 8, 8, n_l), axis=0)
        st["m"] = m_j if j == 0 else jnp.maximum(st["m"], m_j)

    def value_chunk(u, jv):
        hp, n = units[u]
        st = state[u]
        if jv == 0:
            st["m"] = jnp.max(st["m"], axis=0, keepdims=True)
            st["v"] = band_tile(v_ref, hp, n)
        p_v = [_bf16(jnp.exp2(st["s"][j] - st["m"])) for j in range(jv * per_val, (jv + 1) * per_val)]
        lo = jv * VAL_CHUNK
        if lo < n_k:
            v_j = st["v"][lo:lo + VAL_CHUNK]
        else:
            v_j = vc_ref[0, lo - n_k:lo - n_k + VAL_CHUNK, hp * LANES:(hp + 1) * LANES]
        v_ext = jnp.concatenate([v_j.T, ones_rows], axis=0)
        o_j = _dot(v_ext, jnp.concatenate(p_v, axis=0))
        st["o"] = o_j if jv == 0 else st["o"] + o_j

    def finish(u):
        hp, n = units[u]
        st = state[u]
        o_t = st["o"]
        inv_l = 1.0 / o_t[LANES:LANES + 1, :]
        o_t = jnp.concatenate([o_t[:HEAD_DIM, :n_q] * inv_l[:, :n_q],
                               o_t[HEAD_DIM:LANES, n_q:] * inv_l[:, n_q:]], axis=0)
        o_ref[0, :, n * Q_COL_BLOCK:(n + 1) * Q_COL_BLOCK, hp * LANES:(hp + 1) * LANES] = (
            _bf16(o_t.T).reshape(Q_ROWS, Q_COL_BLOCK, LANES))
        st.clear()

    n_val_chunks = n_key_chunks // per_val
    n_slots = len(units) * n_val_chunks
    for slot in range(n_slots + n_val_chunks + VALUE_LAG):
        if slot < n_slots:
            u, jv = divmod(slot, n_val_chunks)
            for j in range(jv * per_val, (jv + 1) * per_val):
                score_chunk(u, j)
        done = slot - n_val_chunks - VALUE_LAG
        if done >= 0:
            u, jv = divmod(done, n_val_chunks)
            value_chunk(u, jv)
            if jv == n_val_chunks - 1:
                finish(u)


def _attn_call(q, k, v, k_c, v_c, bias):
    b, s, _ = q.shape
    rows = s // GRID_W
    n_i = rows // Q_ROWS
    l = k_c.shape[1]
    tok = Q_ROWS * GRID_W

    def row_class(i):
        return jnp.where(i == 0, 0, jnp.where(i == n_i - 1, 2, 1))

    assert (BAND_ROWS * K_COL_BLOCK) % VAL_CHUNK == 0 and l % VAL_CHUNK == 0 and VAL_CHUNK % KEY_CHUNK == 0
    whole_seq = pl.Buffered(2)
    return pl.pallas_call(
        functools.partial(_attn_kernel, rows=rows),
        grid=(b, n_i),
        in_specs=[pl.BlockSpec((1, tok, ATTN_WIDTH), lambda bi, i: (bi, i, 0)),
                  pl.BlockSpec((1, s, ATTN_WIDTH), lambda bi, i: (bi, 0, 0), pipeline_mode=whole_seq),
                  pl.BlockSpec((1, s, ATTN_WIDTH), lambda bi, i: (bi, 0, 0), pipeline_mode=whole_seq),
                  pl.BlockSpec((1, l, ATTN_WIDTH), lambda bi, i: (bi, 0, 0)),
                  pl.BlockSpec((1, l, ATTN_WIDTH), lambda bi, i: (bi, 0, 0)),
                  pl.BlockSpec((1, N_HEAD_PAIRS, N_COL_BLOCKS, BAND_ROWS * K_COL_BLOCK,
                                HEAD_PAIR * Q_ROWS * Q_COL_BLOCK),
                               lambda bi, i: (row_class(i), 0, 0, 0, 0))],
        out_specs=pl.BlockSpec((1, Q_ROWS, GRID_W, ATTN_WIDTH), lambda bi, i: (bi, i, 0, 0)),
        out_shape=jax.ShapeDtypeStruct((b, rows, GRID_W, ATTN_WIDTH), jnp.bfloat16),
        compiler_params=pltpu.CompilerParams(vmem_limit_bytes=VMEM_LIMIT),
        name="attn",
    )(q, k, v, k_c, v_c, bias)


N_RPB_ROWS = 2 * WIN_ROWS - 1
N_RPB_COLS = 2 * WIN_COLS - 1
ROW_CLASS_STEPS = (0, 1, -1)


def _row_class_geometry(i, rows):
    start = min(max(i * Q_ROWS - WIN_ROWS // 2, 0), rows - BAND_ROWS)
    r0 = [min(max(i * Q_ROWS + a - WIN_ROWS // 2, 0), rows - WIN_ROWS) - start for a in range(Q_ROWS)]
    return start, start - i * Q_ROWS + WIN_ROWS - 1, r0


def _bias_kernel(rpb_ref, o_ref, toep_ref, *, rows):
    hp = pl.program_id(0)
    n = pl.program_id(1)
    c0 = jnp.clip(n * Q_COL_BLOCK - WIN_COLS // 2, 0, GRID_W - K_COL_BLOCK)
    kc = c0 + jax.lax.broadcasted_iota(jnp.int32, (K_COL_BLOCK, LANES), 0)
    lane = jax.lax.broadcasted_iota(jnp.int32, (K_COL_BLOCK, LANES), 1)
    q_row = lane // Q_COL_BLOCK
    qc = n * Q_COL_BLOCK + lane % Q_COL_BLOCK
    q_col0 = jnp.clip(qc - WIN_COLS // 2, 0, GRID_W - WIN_COLS)
    col_valid = (kc >= q_col0) & (kc < q_col0 + WIN_COLS)
    cidx = jnp.clip(kc - qc + WIN_COLS - 1, 0, N_RPB_COLS - 1)
    masked = jnp.full((K_COL_BLOCK, LANES), NEG_INF, jnp.float32)
    n_i = rows // Q_ROWS
    for e in range(HEAD_PAIR):
        base = (hp * HEAD_PAIR + e) * N_RPB_ROWS
        for ri in range(N_RPB_ROWS):
            rpb_row = jnp.broadcast_to(rpb_ref[pl.ds(base + ri, 1), :], (K_COL_BLOCK, LANES))
            picked = jnp.take_along_axis(rpb_row, cidx, axis=1)
            toep_ref[ri] = jnp.where(col_valid, picked * LOG2E, masked)
        for rc, step in enumerate(ROW_CLASS_STEPS):
            _, ridx0, r0 = _row_class_geometry(step % n_i, rows)
            for ar in range(BAND_ROWS):
                slab = masked
                for a in range(Q_ROWS):
                    if 0 <= ar - r0[a] < WIN_ROWS:
                        slab = jnp.where(q_row == a, toep_ref[ar - a + ridx0], slab)
                o_ref[rc, 0, 0, ar * K_COL_BLOCK:(ar + 1) * K_COL_BLOCK, e * LANES:(e + 1) * LANES] = slab


def _attn_bias_table(rpb, rows):
    assert rows // Q_ROWS >= 3 and rows % Q_ROWS == 0
    assert Q_ROWS * Q_COL_BLOCK == LANES
    n_q, n_k = HEAD_PAIR * Q_ROWS * Q_COL_BLOCK, BAND_ROWS * K_COL_BLOCK
    n_rc = len(ROW_CLASS_STEPS)
    return pl.pallas_call(
        functools.partial(_bias_kernel, rows=rows),
        grid=(N_HEAD_PAIRS, N_COL_BLOCKS),
        in_specs=[pl.BlockSpec((N_HEADS * N_RPB_ROWS, LANES), lambda hp, n: (0, 0))],
        out_specs=pl.BlockSpec((n_rc, 1, 1, n_k, n_q), lambda hp, n: (0, hp, n, 0, 0)),
        out_shape=jax.ShapeDtypeStruct((n_rc, N_HEAD_PAIRS, N_COL_BLOCKS, n_k, n_q), jnp.float32),
        scratch_shapes=[pltpu.VMEM((N_RPB_ROWS, K_COL_BLOCK, LANES), jnp.float32)],
        name="bias",
    )(jnp.pad(rpb.reshape(N_HEADS * N_RPB_ROWS, N_RPB_COLS), ((0, 0), (0, LANES - N_RPB_COLS))))


def _merge_kernel(x_ref, mod_ref, gpre_ref, gpost_ref, attn_ref, p_ref,
                  wg_ref, bg_ref, pw_ref, ps_ref, wa_ref, wb_ref, wo_ref, o_ref):
    n_sub = x_ref.shape[1] // SUB_ROWS
    mod = mod_ref[0]
    g_in = gpre_ref[...] * (1.0 + mod[:, D_MODEL:2 * D_MODEL])
    g_out = gpost_ref[...] * mod[:, 2 * D_MODEL:3 * D_MODEL]
    n_chunks = D_MODEL // GATE_CHUNK
    lead = min(3, n_chunks)
    state = [dict() for _ in range(n_sub)]

    def gate_pair(st, c):
        lo = c * GATE_CHUNK
        g_a = _sigmoid(_dot(st["h"], wg_ref[:, lo:lo + GATE_CHUNK]) + bg_ref[:, lo:lo + GATE_CHUNK])
        lo += D_MODEL
        g_b = _sigmoid(_dot(st["h"], wg_ref[:, lo:lo + GATE_CHUNK]) + bg_ref[:, lo:lo + GATE_CHUNK])
        st["gates"].append((g_a, g_b))

    def out_partial(st, c):
        g_a, g_b = st["gates"][c]
        sl = slice(c * GATE_CHUNK, (c + 1) * GATE_CHUNK)
        part = _dot(_bf16(g_a * st["ya"][:, sl] + g_b * st["yb"][:, sl]), wo_ref[sl, :])
        st["y"] = part if c == 0 else st["y"] + part

    def stage(k, s):
        st = state[k]
        rows = slice(k * SUB_ROWS, (k + 1) * SUB_ROWS)
        if s == 0:
            st["ya"] = _dot(attn_ref[0, rows, :], wa_ref[...])
            mixed = [_dot(p_ref[0, rows, g * POOL_GROUP:(g + 1) * POOL_GROUP], pw_ref[g])
                     for g in range(N_POOL_GROUPS)]
            st["yb"] = _dot(_bf16(jnp.concatenate(mixed, axis=-1) * ps_ref[...]), wb_ref[...])
        elif s <= lead:
            if s == 1:
                st["h"] = _bf16(_rms(x_ref[0, rows, :], g_in) + mod[:, :D_MODEL])
                st["gates"] = []
            gate_pair(st, s - 1)
        else:
            c = s - lead - 1
            if c + lead < n_chunks:
                gate_pair(st, c + lead)
            out_partial(st, c)
            if c == n_chunks - 1:
                o_ref[0, rows, :] = x_ref[0, rows, :] + _rms(st["y"], g_out)
                st.clear()

    n_stages = lead + 1 + n_chunks
    for step in range(n_stages + MERGE_LAG * (n_sub - 1)):
        for k in range(n_sub):
            s = step - k * MERGE_LAG
            if 0 <= s < n_stages:
                stage(k, s)


def _merge_call(x, mod3, g_pre, g_post, attn, pooled, w_in, b_gate, pool_w, pool_scale, w_a, w_b, w_o, tile):
    b, s, d = x.shape
    assert w_in.shape[1] == 2 * GATE_WIDTH

    def const(shape):
        return pl.BlockSpec(shape, lambda i, t: (0,) * len(shape))

    return pl.pallas_call(
        _merge_kernel,
        grid=(b, s // tile),
        in_specs=[pl.BlockSpec((1, tile, d), lambda i, t: (i, t, 0)),
                  pl.BlockSpec((1, 1, N_MOD * d), lambda i, t: (i, 0, 0)),
                  const((1, d)), const((1, d)),
                  pl.BlockSpec((1, tile, ATTN_WIDTH), lambda i, t: (i, t, 0)),
                  pl.BlockSpec((1, tile, POOL_WIDTH), lambda i, t: (i, t, 0)),
                  pl.BlockSpec((d, GATE_WIDTH), lambda i, t: (0, 1)), const((1, GATE_WIDTH)),
                  const((N_POOL_GROUPS, POOL_GROUP, POOL_GROUP)), const((1, POOL_WIDTH)),
                  const((ATTN_WIDTH, d)), const((POOL_WIDTH, d)), const((d, d))],
        out_specs=pl.BlockSpec((1, tile, d), lambda i, t: (i, t, 0)),
        out_shape=jax.ShapeDtypeStruct((b, s, d), jnp.float32),
        compiler_params=pltpu.CompilerParams(vmem_limit_bytes=VMEM_LIMIT),
        name="merge",
    )(x, mod3, g_pre, g_post, attn, pooled, w_in, b_gate, pool_w, pool_scale, w_a, w_b, w_o)


def _ffn_kernel(x_ref, mod_ref, gpre_ref, gpost_ref, wu_ref, wd_ref, o_ref, *, d_ff, chunk):
    mod = mod_ref[0]
    g_in = gpre_ref[...] * (1.0 + mod[:, 4 * D_MODEL:5 * D_MODEL])
    g_out = gpost_ref[...] * mod[:, 5 * D_MODEL:]
    subs = list(range(0, x_ref.shape[1], SUB_ROWS))
    n_chunks = d_ff // chunk
    lag = n_chunks // 2
    hs, accs = {}, {}

    def ffn_chunk(r, c):
        if c == 0:
            hs[r] = _bf16(_rms(x_ref[0, r:r + SUB_ROWS, :], g_in) + mod[:, 3 * D_MODEL:4 * D_MODEL])
        c0 = c * chunk
        gate = _dot(hs[r], wu_ref[:, c0:c0 + chunk])
        up = _dot(hs[r], wu_ref[:, d_ff + c0:d_ff + c0 + chunk])
        part = _dot(_bf16(gate * _sigmoid(gate) * up), wd_ref[c0:c0 + chunk, :])
        accs[r] = part if c == 0 else accs[r] + part
        if c == n_chunks - 1:
            o_ref[0, r:r + SUB_ROWS, :] = x_ref[0, r:r + SUB_ROWS, :] + _rms(accs[r], g_out)

    for step in range(n_chunks + lag * (len(subs) - 1)):
        for idx, r in enumerate(subs):
            c = step - idx * lag
            if 0 <= c < n_chunks:
                ffn_chunk(r, c)


def _ffn_call(x, mod3, g_pre, g_post, w_up, w_down, tile):
    b, s, d = x.shape
    d_ff = w_down.shape[0]
    chunk = 2 * LANES
    assert d_ff % chunk == 0

    def const(shape):
        return pl.BlockSpec(shape, lambda i, t: (0,) * len(shape))

    return pl.pallas_call(
        functools.partial(_ffn_kernel, d_ff=d_ff, chunk=chunk),
        grid=(b, s // tile),
        in_specs=[pl.BlockSpec((1, tile, d), lambda i, t: (i, t, 0)),
                  pl.BlockSpec((1, 1, N_MOD * d), lambda i, t: (i, 0, 0)),
                  const((1, d)), const((1, d)),
                  pl.BlockSpec((d, 2 * d_ff), lambda i, t: (0, 0), pipeline_mode=pl.Buffered(1)),
                  pl.BlockSpec((d_ff, d), lambda i, t: (0, 0), pipeline_mode=pl.Buffered(1))],
        out_specs=pl.BlockSpec((1, tile, d), lambda i, t: (i, t, 0)),
        out_shape=jax.ShapeDtypeStruct((b, s, d), jnp.float32),
        compiler_params=pltpu.CompilerParams(vmem_limit_bytes=VMEM_LIMIT),
        name="ffn",
    )(x, mod3, g_pre, g_post, w_up, w_down)


def _rope_tables(n_tok):
    t = np.arange(n_tok)
    pos = np.stack([t // GRID_W, t % GRID_W], axis=-1).astype(np.float64)
    inv = ROPE_THETA ** (-np.arange(ROPE_FREQS, dtype=np.float64) / ROPE_FREQS)
    ang = pos[:, :, None] * inv
    cos, sin = np.cos(ang), np.sin(ang)
    cos_h = np.concatenate([cos[:, 0], cos[:, 0], cos[:, 1], cos[:, 1]], axis=-1)
    sin_h = np.concatenate([-sin[:, 0], sin[:, 0], -sin[:, 1], sin[:, 1]], axis=-1)
    cos_p, sin_p = np.tile(cos_h, (1, HEAD_PAIR)), np.tile(sin_h, (1, HEAD_PAIR))
    q_scale = HEAD_DIM ** -0.5 * LOG2E
    return jnp.asarray(np.stack([cos_p * q_scale, sin_p * q_scale, cos_p, sin_p]).astype(np.float32))


def kernel(x, c, ctx, c_ctx, w_ada, b_ada, g_pre_mix, g_post_mix, g_pre_ffn, g_post_ffn, w_in, b_gate, rpb,
           pool_w, pool_scale, w_proj_a, w_proj_b, w_out, w_up, w_down):
    b, s, d = x.shape
    depth = w_ada.shape[0]
    assert depth == 1, "context stream updates are only needed for depth > 1"
    rows = s // GRID_W
    a = ATTN_WIDTH

    c8 = jnp.zeros((8, d), jnp.float32).at[:b].set(c).at[b].set(c_ctx)
    mod = _mod_call(c8, w_ada[0], b_ada[0])
    mod3 = mod[:b].reshape(b, 1, N_MOD * d)
    mod_ctx = mod[b:b + 1, :2 * d].reshape(1, 1, 2 * d)

    w_in_b = _bf16(w_in[0])
    g_pre = g_pre_mix[0].reshape(1, d)
    k_c, v_c = _ctx_kv_call(ctx, mod_ctx, g_pre, w_in_b)
    q, k, v, pooled = _proj_call(x, mod3, g_pre, w_in_b, _rope_tables(s), tile=PROJ_TILE)
    bias = _attn_bias_table(rpb[0], rows)
    attn = _attn_call(q, k, v, k_c, v_c, bias).reshape(b, s, a)
    x1 = _merge_call(x, mod3, g_pre, g_post_mix[0].reshape(1, d), attn, pooled, w_in_b, b_gate[0].reshape(1, -1),
                     _bf16(pool_w[0]), pool_scale[0].reshape(1, -1), _bf16(w_proj_a[0]), _bf16(w_proj_b[0]),
                     _bf16(w_out[0]), tile=2 * SUB_ROWS)
    return _ffn_call(x1, mod3, g_pre_ffn[0].reshape(1, d), g_post_ffn[0].reshape(1, d),
                     _bf16(w_up[0]), _bf16(w_down[0]), tile=2 * SUB_ROWS)
```

```python
import functools

import jax
import jax.numpy as jnp
import numpy as np
from jax.experimental import pallas as pl
from jax.experimental.pallas import tpu as pltpu

D_MODEL = 1024
GRID_W = 64
N_HEADS = 8
HEAD_DIM = 64
ATTN_WIDTH = N_HEADS * HEAD_DIM
POOL_WINDOWS = (2, 4, 8, 16)
N_POOL_GROUPS = len(POOL_WINDOWS)
POOL_WIDTH = 512
POOL_GROUP = POOL_WIDTH // N_POOL_GROUPS
GATE_WIDTH = 2 * D_MODEL
WIN_ROWS = 8
WIN_COLS = 16
Q_COL_BLOCK = 16
K_COL_BLOCK = Q_COL_BLOCK + WIN_COLS
ROPE_FREQS = HEAD_DIM // 4
ROPE_THETA = 10000.0
N_MOD = 6
EPS = 1e-6
NEG_INF = -1e30

LANES = 128
BF16_SUBLANES = 16
LOG2E = 1.4426950408889634
HEAD_PAIR = LANES // HEAD_DIM
N_HEAD_PAIRS = N_HEADS // HEAD_PAIR
POOL_HALO = max(POOL_WINDOWS) // 2
Q_ROWS = 8
BAND_ROWS = 16
KEY_CHUNK = 256
VAL_CHUNK = 256
GATE_CHUNK = 256
PROJ_TILE = 1024
PROJ_SUB_ROWS = 512
SUB_ROWS = 512
MERGE_LAG = 4
VALUE_LAG = 2
N_COL_BLOCKS = GRID_W // Q_COL_BLOCK
VMEM_LIMIT = 56 * 1024 * 1024


def _bf16(a):
    return a.astype(jnp.bfloat16)


def _dot(a, b):
    return jnp.dot(a, b, preferred_element_type=jnp.float32)


def _rms(xf, g):
    return (xf * jax.lax.rsqrt(jnp.mean(xf * xf, axis=-1, keepdims=True) + EPS)) * g


def _sigmoid(z):
    return 1.0 / (1.0 + jnp.exp2(z * -LOG2E))


def _mod_kernel(c_ref, w_ref, b_ref, o_ref):
    c = c_ref[...]
    s = c * _sigmoid(c)
    o_ref[...] = _dot(_bf16(s), _bf16(w_ref[...])) + b_ref[...]


def _mod_call(c8, w_ada, b_ada):
    n = w_ada.shape[1]
    bn = D_MODEL
    return pl.pallas_call(
        _mod_kernel,
        grid=(n // bn,),
        in_specs=[pl.BlockSpec((8, D_MODEL), lambda j: (0, 0)),
                  pl.BlockSpec((D_MODEL, bn), lambda j: (0, j)),
                  pl.BlockSpec((1, bn), lambda j: (0, j))],
        out_specs=pl.BlockSpec((8, bn), lambda j: (0, j)),
        out_shape=jax.ShapeDtypeStruct((8, n), jnp.float32),
        name="mod",
    )(c8, w_ada, b_ada.reshape(1, n))


def _ctx_kv_kernel(ctx_ref, mod_ref, g_ref, wk_ref, wv_ref, k_ref, v_ref):
    mod = mod_ref[0]
    gain = g_ref[...] * (1.0 + mod[:, D_MODEL:2 * D_MODEL])
    h = _bf16(_rms(ctx_ref[0], gain) + mod[:, :D_MODEL])
    k_ref[0] = _bf16(_dot(h, wk_ref[...]))
    v_ref[0] = _bf16(_dot(h, wv_ref[...]))


def _ctx_kv_call(ctx, mod_ctx, g_pre, w_in):
    b, l, d = ctx.shape
    out = jax.ShapeDtypeStruct((b, l, ATTN_WIDTH), jnp.bfloat16)
    return pl.pallas_call(
        _ctx_kv_kernel,
        grid=(b,),
        in_specs=[pl.BlockSpec((1, l, d), lambda i: (i, 0, 0)),
                  pl.BlockSpec((1, 1, 2 * d), lambda i: (0, 0, 0)),
                  pl.BlockSpec((1, d), lambda i: (0, 0)),
                  pl.BlockSpec((d, ATTN_WIDTH), lambda i: (0, 1)),
                  pl.BlockSpec((d, ATTN_WIDTH), lambda i: (0, 2))],
        out_specs=[pl.BlockSpec((1, l, ATTN_WIDTH), lambda i: (i, 0, 0))] * 2,
        out_shape=[out, out],
        name="ctx_kv",
    )(ctx, mod_ctx, g_pre, w_in, w_in)


def _rope(a, cos, sin_signed, first_half):
    swapped = jnp.where(first_half, pltpu.roll(a, LANES - ROPE_FREQS, 1), pltpu.roll(a, ROPE_FREQS, 1))
    return a * cos + swapped * sin_signed


def _window_means_minus_token(ext, tok0, seq):
    halo = POOL_HALO
    n_ext = ext.shape[0]
    n_tok = n_ext - 2 * halo
    tok_head = tok0 + jax.lax.broadcasted_iota(jnp.int32, (halo, POOL_GROUP), 0)
    tok_tail = tok_head + (n_tok - halo)
    pooled = []
    for g, w in enumerate(POOL_WINDOWS):
        e = ext[:, g * POOL_GROUP:(g + 1) * POOL_GROUP]
        acc = e
        span = 1
        while 2 * span < w:
            acc = acc + pltpu.roll(acc, n_ext - span, 0)
            span *= 2
        acc = acc + pltpu.roll(acc, span, 0)

        def window_count(tok):
            return (jnp.minimum(tok + (w - w // 2), seq) - jnp.maximum(tok - w // 2, 0)).astype(jnp.float32)

        mean = jnp.concatenate([acc[halo:2 * halo] / window_count(tok_head),
                                acc[2 * halo:n_tok] * (1.0 / w),
                                acc[n_tok:halo + n_tok] / window_count(tok_tail)], axis=0)
        pooled.append(mean - e[halo:halo + n_tok])
    return jnp.concatenate(pooled, axis=-1)


def _proj_kernel(x_ref, xprev_ref, xnext_ref, mod_ref, g_ref, w_ref, rope_ref,
                 q_ref, k_ref, v_ref, p_ref, *, seq):
    t = pl.program_id(1)
    tile = x_ref.shape[1]
    n_sub = tile // PROJ_SUB_ROWS
    halo = POOL_HALO
    mod = mod_ref[0]
    gain = g_ref[...] * (1.0 + mod[:, D_MODEL:2 * D_MODEL])
    shift = mod[:, :D_MODEL]
    lane = jax.lax.broadcasted_iota(jnp.int32, (PROJ_SUB_ROWS, LANES), 1)
    first_half = (lane % (2 * ROPE_FREQS)) < ROPE_FREQS
    hs = {}

    def pool_part(k):
        r = k * PROJ_SUB_ROWS
        hs[k] = _bf16(_rms(x_ref[0, r:r + PROJ_SUB_ROWS, :], gain) + shift)
        x_prev = xprev_ref[0, 0] if k == 0 else x_ref[0, r - halo:r, :]
        x_next = xnext_ref[0, 0] if k == n_sub - 1 else x_ref[0, r + PROJ_SUB_ROWS:r + PROJ_SUB_ROWS + halo, :]
        h_halo = _bf16(_rms(jnp.concatenate([x_prev, x_next], axis=0), gain) + shift)
        p_all = _dot(jnp.concatenate([hs[k], h_halo], axis=0), w_ref[:, 3 * ATTN_WIDTH:])
        p_prev = p_all[PROJ_SUB_ROWS:PROJ_SUB_ROWS + halo]
        p_next = p_all[PROJ_SUB_ROWS + halo:]
        if k == 0:
            p_prev = jnp.where(t == 0, 0.0, p_prev)
        if k == n_sub - 1:
            p_next = jnp.where(t == pl.num_programs(1) - 1, 0.0, p_next)
        ext = jnp.concatenate([p_prev, p_all[:PROJ_SUB_ROWS], p_next], axis=0)
        p_ref[0, r:r + PROJ_SUB_ROWS, :] = _bf16(_window_means_minus_token(ext, t * tile + r, seq))

    def qkv_part(k):
        r = k * PROJ_SUB_ROWS
        rows = slice(r, r + PROJ_SUB_ROWS)
        proj = _dot(hs.pop(k), w_ref[:, :3 * ATTN_WIDTH])
        cos_q, sin_q, cos_k, sin_k = (rope_ref[i, rows, :] for i in range(4))
        for j in range(N_HEAD_PAIRS):
            sl = slice(j * LANES, (j + 1) * LANES)
            q_ref[0, rows, sl] = _bf16(_rope(proj[:, sl], cos_q, sin_q, first_half))
            ksl = slice(ATTN_WIDTH + j * LANES, ATTN_WIDTH + (j + 1) * LANES)
            k_ref[0, rows, sl] = _bf16(_rope(proj[:, ksl], cos_k, sin_k, first_half))
        v_ref[0, rows, :] = _bf16(proj[:, 2 * ATTN_WIDTH:3 * ATTN_WIDTH])

    for k in range(n_sub):
        pool_part(k)
        qkv_part(k)


def _proj_call(x, mod3, g_pre, w_in, rope, tile):
    b, s, d = x.shape
    n = 3 * ATTN_WIDTH + POOL_WIDTH
    halo = POOL_HALO
    assert POOL_WIDTH == ATTN_WIDTH and 2 * halo == BF16_SUBLANES and tile % BF16_SUBLANES == 0
    x4 = x.reshape(b, s // halo, halo, d)
    n_halo = s // halo
    per = tile // halo
    act = jax.ShapeDtypeStruct((b, s, ATTN_WIDTH), jnp.bfloat16)
    act_spec = pl.BlockSpec((1, tile, ATTN_WIDTH), lambda i, t: (i, t, 0))
    return pl.pallas_call(
        functools.partial(_proj_kernel, seq=s),
        grid=(b, s // tile),
        in_specs=[pl.BlockSpec((1, tile, d), lambda i, t: (i, t, 0)),
                  pl.BlockSpec((1, 1, halo, d), lambda i, t: (i, jnp.maximum(t * per - 1, 0), 0, 0)),
                  pl.BlockSpec((1, 1, halo, d), lambda i, t: (i, jnp.minimum((t + 1) * per, n_halo - 1), 0, 0)),
                  pl.BlockSpec((1, 1, N_MOD * d), lambda i, t: (i, 0, 0)),
                  pl.BlockSpec((1, d), lambda i, t: (0, 0)),
                  pl.BlockSpec((d, n), lambda i, t: (0, 0)),
                  pl.BlockSpec((4, tile, LANES), lambda i, t: (0, t, 0))],
        out_specs=[act_spec] * 4,
        out_shape=[act] * 4,
        compiler_params=pltpu.CompilerParams(vmem_limit_bytes=VMEM_LIMIT),
        name="proj",
    )(x, x4, x4, mod3, g_pre, w_in, rope)


def _band_start(i, rows):
    return jnp.clip(i * Q_ROWS - WIN_ROWS // 2, 0, rows - BAND_ROWS)


def _key_col0(n):
    return min(max(n * Q_COL_BLOCK - WIN_COLS // 2, 0), GRID_W - K_COL_BLOCK)


def _attn_kernel(q_ref, k_ref, v_ref, kc_ref, vc_ref, bias_ref, o_ref, *, rows):
    i = pl.program_id(1)
    start = pl.multiple_of(_band_start(i, rows) * GRID_W, GRID_W)
    n_q = Q_ROWS * Q_COL_BLOCK
    n_k = BAND_ROWS * K_COL_BLOCK
    n_l = HEAD_PAIR * n_q
    n_ctx = kc_ref.shape[1]
    first_d = jax.lax.broadcasted_iota(jnp.int32, (LANES, n_q), 0) < HEAD_DIM
    zero = jnp.zeros((), jnp.bfloat16)
    ones_rows = jnp.ones((BF16_SUBLANES, VAL_CHUNK), jnp.bfloat16)
    per_val = VAL_CHUNK // KEY_CHUNK
    n_key_chunks = (n_k + n_ctx) // KEY_CHUNK
    units = [(hp, n) for hp in range(N_HEAD_PAIRS) for n in range(N_COL_BLOCKS)]
    state = [dict() for _ in units]

    def band_tile(ref, hp, n):
        c0 = _key_col0(n)
        band = ref[0, pl.ds(start, BAND_ROWS * GRID_W), hp * LANES:(hp + 1) * LANES]
        return band.reshape(BAND_ROWS, GRID_W, LANES)[:, c0:c0 + K_COL_BLOCK, :].reshape(n_k, LANES)

    def score_chunk(u, j):
        hp, n = units[u]
        st = state[u]
        if j == 0:
            q_t = q_ref[0, :, hp * LANES:(hp + 1) * LANES].reshape(Q_ROWS, GRID_W, LANES)
            q_t = q_t[:, n * Q_COL_BLOCK:(n + 1) * Q_COL_BLOCK, :].reshape(n_q, LANES).T
            st["q"] = jnp.concatenate([jnp.where(first_d, q_t, zero), jnp.where(first_d, zero, q_t)], axis=1)
            st["k"] = band_tile(k_ref, hp, n)
            st["s"] = []
        lo = j * KEY_CHUNK
        if lo < n_k:
            s_j = _dot(st["k"][lo:lo + KEY_CHUNK], st["q"]) + bias_ref[0, hp, n, lo:lo + KEY_CHUNK, :]
        else:
            s_j = _dot(kc_ref[0, lo - n_k:lo - n_k + KEY_CHUNK, hp * LANES:(hp + 1) * LANES], st["q"])
        st["s"].append(s_j)
        m_j = jnp.max(s_j.reshape(KEY_CHUNK // 8, 8, n_l), axis=0)
        st["m"] = m_j if j == 0 else jnp.maximum(st["m"], m_j)

    def value_chunk(u, jv):
        hp, n = units[u]
        st = state[u]
        if jv == 0:
            st["m"] = jnp.max(st["m"], axis=0, keepdims=True)
            st["v"] = band_tile(v_ref, hp, n)
        p_v = [_bf16(jnp.exp2(st["s"][j] - st["m"])) for j in range(jv * per_val, (jv + 1) * per_val)]
        lo = jv * VAL_CHUNK
        if lo < n_k:
            v_j = st["v"][lo:lo + VAL_CHUNK]
        else:
            v_j = vc_ref[0, lo - n_k:lo - n_k + VAL_CHUNK, hp * LANES:(hp + 1) * LANES]
        v_ext = jnp.concatenate([v_j.T, ones_rows], axis=0)
        o_j = _dot(v_ext, jnp.concatenate(p_v, axis=0))
        st["o"] = o_j if jv == 0 else st["o"] + o_j

    def finish(u):
        hp, n = units[u]
        st = state[u]
        o_t = st["o"]
        inv_l = 1.0 / o_t[LANES:LANES + 1, :]
        o_t = jnp.concatenate([o_t[:HEAD_DIM, :n_q] * inv_l[:, :n_q],
                               o_t[HEAD_DIM:LANES, n_q:] * inv_l[:, n_q:]], axis=0)
        o_ref[0, :, n * Q_COL_BLOCK:(n + 1) * Q_COL_BLOCK, hp * LANES:(hp + 1) * LANES] = (
            _bf16(o_t.T).reshape(Q_ROWS, Q_COL_BLOCK, LANES))
        st.clear()

    n_val_chunks = n_key_chunks // per_val
    n_slots = len(units) * n_val_chunks
    for slot in range(n_slots + n_val_chunks + VALUE_LAG):
        if slot < n_slots:
            u, jv = divmod(slot, n_val_chunks)
            for j in range(jv * per_val, (jv + 1) * per_val):
                score_chunk(u, j)
        done = slot - n_val_chunks - VALUE_LAG
        if done >= 0:
            u, jv = divmod(done, n_val_chunks)
            value_chunk(u, jv)
            if jv == n_val_chunks - 1:
                finish(u)


def _attn_call(q, k, v, k_c, v_c, bias):
    b, s, _ = q.shape
    rows = s // GRID_W
    n_i = rows // Q_ROWS
    l = k_c.shape[1]
    tok = Q_ROWS * GRID_W

    def row_class(i):
        return jnp.where(i == 0, 0, jnp.where(i == n_i - 1, 2, 1))

    assert (BAND_ROWS * K_COL_BLOCK) % VAL_CHUNK == 0 and l % VAL_CHUNK == 0 and VAL_CHUNK % KEY_CHUNK == 0
    whole_seq = pl.Buffered(2)
    return pl.pallas_call(
        functools.partial(_attn_kernel, rows=rows),
        grid=(b, n_i),
        in_specs=[pl.BlockSpec((1, tok, ATTN_WIDTH), lambda bi, i: (bi, i, 0)),
                  pl.BlockSpec((1, s, ATTN_WIDTH), lambda bi, i: (bi, 0, 0), pipeline_mode=whole_seq),
                  pl.BlockSpec((1, s, ATTN_WIDTH), lambda bi, i: (bi, 0, 0), pipeline_mode=whole_seq),
                  pl.BlockSpec((1, l, ATTN_WIDTH), lambda bi, i: (bi, 0, 0)),
                  pl.BlockSpec((1, l, ATTN_WIDTH), lambda bi, i: (bi, 0, 0)),
                  pl.BlockSpec((1, N_HEAD_PAIRS, N_COL_BLOCKS, BAND_ROWS * K_COL_BLOCK,
                                HEAD_PAIR * Q_ROWS * Q_COL_BLOCK),
                               lambda bi, i: (row_class(i), 0, 0, 0, 0))],
        out_specs=pl.BlockSpec((1, Q_ROWS, GRID_W, ATTN_WIDTH), lambda bi, i: (bi, i, 0, 0)),
        out_shape=jax.ShapeDtypeStruct((b, rows, GRID_W, ATTN_WIDTH), jnp.bfloat16),
        compiler_params=pltpu.CompilerParams(vmem_limit_bytes=VMEM_LIMIT),
        name="attn",
    )(q, k, v, k_c, v_c, bias)


N_RPB_ROWS = 2 * WIN_ROWS - 1
N_RPB_COLS = 2 * WIN_COLS - 1
ROW_CLASS_STEPS = (0, 1, -1)


def _row_class_geometry(i, rows):
    start = min(max(i * Q_ROWS - WIN_ROWS // 2, 0), rows - BAND_ROWS)
    r0 = [min(max(i * Q_ROWS + a - WIN_ROWS // 2, 0), rows - WIN_ROWS) - start for a in range(Q_ROWS)]
    return start, start - i * Q_ROWS + WIN_ROWS - 1, r0


def _bias_kernel(rpb_ref, o_ref, toep_ref, *, rows):
    hp = pl.program_id(0)
    n = pl.program_id(1)
    c0 = jnp.clip(n * Q_COL_BLOCK - WIN_COLS // 2, 0, GRID_W - K_COL_BLOCK)
    kc = c0 + jax.lax.broadcasted_iota(jnp.int32, (K_COL_BLOCK, LANES), 0)
    lane = jax.lax.broadcasted_iota(jnp.int32, (K_COL_BLOCK, LANES), 1)
    q_row = lane // Q_COL_BLOCK
    qc = n * Q_COL_BLOCK + lane % Q_COL_BLOCK
    q_col0 = jnp.clip(qc - WIN_COLS // 2, 0, GRID_W - WIN_COLS)
    col_valid = (kc >= q_col0) & (kc < q_col0 + WIN_COLS)
    cidx = jnp.clip(kc - qc + WIN_COLS - 1, 0, N_RPB_COLS - 1)
    masked = jnp.full((K_COL_BLOCK, LANES), NEG_INF, jnp.float32)
    n_i = rows // Q_ROWS
    for e in range(HEAD_PAIR):
        base = (hp * HEAD_PAIR + e) * N_RPB_ROWS
        for ri in range(N_RPB_ROWS):
            rpb_row = jnp.broadcast_to(rpb_ref[pl.ds(base + ri, 1), :], (K_COL_BLOCK, LANES))
            picked = jnp.take_along_axis(rpb_row, cidx, axis=1)
            toep_ref[ri] = jnp.where(col_valid, picked * LOG2E, masked)
        for rc, step in enumerate(ROW_CLASS_STEPS):
            _, ridx0, r0 = _row_class_geometry(step % n_i, rows)
            for ar in range(BAND_ROWS):
                slab = masked
                for a in range(Q_ROWS):
                    if 0 <= ar - r0[a] < WIN_ROWS:
                        slab = jnp.where(q_row == a, toep_ref[ar - a + ridx0], slab)
                o_ref[rc, 0, 0, ar * K_COL_BLOCK:(ar + 1) * K_COL_BLOCK, e * LANES:(e + 1) * LANES] = slab


def _attn_bias_table(rpb, rows):
    assert rows // Q_ROWS >= 3 and rows % Q_ROWS == 0
    assert Q_ROWS * Q_COL_BLOCK == LANES
    n_q, n_k = HEAD_PAIR * Q_ROWS * Q_COL_BLOCK, BAND_ROWS * K_COL_BLOCK
    n_rc = len(ROW_CLASS_STEPS)
    return pl.pallas_call(
        functools.partial(_bias_kernel, rows=rows),
        grid=(N_HEAD_PAIRS, N_COL_BLOCKS),
        in_specs=[pl.BlockSpec((N_HEADS * N_RPB_ROWS, LANES), lambda hp, n: (0, 0))],
        out_specs=pl.BlockSpec((n_rc, 1, 1, n_k, n_q), lambda hp, n: (0, hp, n, 0, 0)),
        out_shape=jax.ShapeDtypeStruct((n_rc, N_HEAD_PAIRS, N_COL_BLOCKS, n_k, n_q), jnp.float32),
        scratch_shapes=[pltpu.VMEM((N_RPB_ROWS, K_COL_BLOCK, LANES), jnp.float32)],
        name="bias",
    )(jnp.pad(rpb.reshape(N_HEADS * N_RPB_ROWS, N_RPB_COLS), ((0, 0), (0, LANES - N_RPB_COLS))))


def _merge_kernel(x_ref, mod_ref, gpre_ref, gpost_ref, attn_ref, p_ref,
                  wg_ref, bg_ref, pw_ref, ps_ref, wa_ref, wb_ref, wo_ref, o_ref):
    n_sub = x_ref.shape[1] // SUB_ROWS
    mod = mod_ref[0]
    g_in = gpre_ref[...] * (1.0 + mod[:, D_MODEL:2 * D_MODEL])
    g_out = gpost_ref[...] * mod[:, 2 * D_MODEL:3 * D_MODEL]
    n_chunks = D_MODEL // GATE_CHUNK
    lead = min(3, n_chunks)
    state = [dict() for _ in range(n_sub)]

    def gate_pair(st, c):
        lo = c * GATE_CHUNK
        g_a = _sigmoid(_dot(st["h"], wg_ref[:, lo:lo + GATE_CHUNK]) + bg_ref[:, lo:lo + GATE_CHUNK])
        lo += D_MODEL
        g_b = _sigmoid(_dot(st["h"], wg_ref[:, lo:lo + GATE_CHUNK]) + bg_ref[:, lo:lo + GATE_CHUNK])
        st["gates"].append((g_a, g_b))

    def out_partial(st, c):
        g_a, g_b = st["gates"][c]
        sl = slice(c * GATE_CHUNK, (c + 1) * GATE_CHUNK)
        part = _dot(_bf16(g_a * st["ya"][:, sl] + g_b * st["yb"][:, sl]), wo_ref[sl, :])
        st["y"] = part if c == 0 else st["y"] + part

    def stage(k, s):
        st = state[k]
        rows = slice(k * SUB_ROWS, (k + 1) * SUB_ROWS)
        if s == 0:
            st["ya"] = _dot(attn_ref[0, rows, :], wa_ref[...])
            mixed = [_dot(p_ref[0, rows, g * POOL_GROUP:(g + 1) * POOL_GROUP], pw_ref[g])
                     for g in range(N_POOL_GROUPS)]
            st["yb"] = _dot(_bf16(jnp.concatenate(mixed, axis=-1) * ps_ref[...]), wb_ref[...])
        elif s <= lead:
            if s == 1:
                st["h"] = _bf16(_rms(x_ref[0, rows, :], g_in) + mod[:, :D_MODEL])
                st["gates"] = []
            gate_pair(st, s - 1)
        else:
            c = s - lead - 1
            if c + lead < n_chunks:
                gate_pair(st, c + lead)
            out_partial(st, c)
            if c == n_chunks - 1:
                o_ref[0, rows, :] = x_ref[0, rows, :] + _rms(st["y"], g_out)
                st.clear()

    n_stages = lead + 1 + n_chunks
    for step in range(n_stages + MERGE_LAG * (n_sub - 1)):
        for k in range(n_sub):
            s = step - k * MERGE_LAG
            if 0 <= s < n_stages:
                stage(k, s)


def _merge_call(x, mod3, g_pre, g_post, attn, pooled, w_in, b_gate, pool_w, pool_scale, w_a, w_b, w_o, tile):
    b, s, d = x.shape
    assert w_in.shape[1] == 2 * GATE_WIDTH

    def const(shape):
        return pl.BlockSpec(shape, lambda i, t: (0,) * len(shape))

    return pl.pallas_call(
        _merge_kernel,
        grid=(b, s // tile),
        in_specs=[pl.BlockSpec((1, tile, d), lambda i, t: (i, t, 0)),
                  pl.BlockSpec((1, 1, N_MOD * d), lambda i, t: (i, 0, 0)),
                  const((1, d)), const((1, d)),
                  pl.BlockSpec((1, tile, ATTN_WIDTH), lambda i, t: (i, t, 0)),
                  pl.BlockSpec((1, tile, POOL_WIDTH), lambda i, t: (i, t, 0)),
                  pl.BlockSpec((d, GATE_WIDTH), lambda i, t: (0, 1)), const((1, GATE_WIDTH)),
                  const((N_POOL_GROUPS, POOL_GROUP, POOL_GROUP)), const((1, POOL_WIDTH)),
                  const((ATTN_WIDTH, d)), const((POOL_WIDTH, d)), const((d, d))],
        out_specs=pl.BlockSpec((1, tile, d), lambda i, t: (i, t, 0)),
        out_shape=jax.ShapeDtypeStruct((b, s, d), jnp.float32),
        compiler_params=pltpu.CompilerParams(vmem_limit_bytes=VMEM_LIMIT),
        name="merge",
    )(x, mod3, g_pre, g_post, attn, pooled, w_in, b_gate, pool_w, pool_scale, w_a, w_b, w_o)


def _ffn_kernel(x_ref, mod_ref, gpre_ref, gpost_ref, wu_ref, wd_ref, o_ref, *, d_ff, chunk):
    mod = mod_ref[0]
    g_in = gpre_ref[...] * (1.0 + mod[:, 4 * D_MODEL:5 * D_MODEL])
    g_out = gpost_ref[...] * mod[:, 5 * D_MODEL:]
    subs = list(range(0, x_ref.shape[1], SUB_ROWS))
    n_chunks = d_ff // chunk
    lag = n_chunks // 2
    hs, accs = {}, {}

    def ffn_chunk(r, c):
        if c == 0:
            hs[r] = _bf16(_rms(x_ref[0, r:r + SUB_ROWS, :], g_in) + mod[:, 3 * D_MODEL:4 * D_MODEL])
        c0 = c * chunk
        gate = _dot(hs[r], wu_ref[:, c0:c0 + chunk])
        up = _dot(hs[r], wu_ref[:, d_ff + c0:d_ff + c0 + chunk])
        part = _dot(_bf16(gate * _sigmoid(gate) * up), wd_ref[c0:c0 + chunk, :])
        accs[r] = part if c == 0 else accs[r] + part
        if c == n_chunks - 1:
            o_ref[0, r:r + SUB_ROWS, :] = x_ref[0, r:r + SUB_ROWS, :] + _rms(accs[r], g_out)

    for step in range(n_chunks + lag * (len(subs) - 1)):
        for idx, r in enumerate(subs):
            c = step - idx * lag
            if 0 <= c < n_chunks:
                ffn_chunk(r, c)


def _ffn_call(x, mod3, g_pre, g_post, w_up, w_down, tile):
    b, s, d = x.shape
    d_ff = w_down.shape[0]
    chunk = 2 * LANES
    assert d_ff % chunk == 0

    def const(shape):
        return pl.BlockSpec(shape, lambda i, t: (0,) * len(shape))

    return pl.pallas_call(
        functools.partial(_ffn_kernel, d_ff=d_ff, chunk=chunk),
        grid=(b, s // tile),
        in_specs=[pl.BlockSpec((1, tile, d), lambda i, t: (i, t, 0)),
                  pl.BlockSpec((1, 1, N_MOD * d), lambda i, t: (i, 0, 0)),
                  const((1, d)), const((1, d)),
                  pl.BlockSpec((d, 2 * d_ff), lambda i, t: (0, 0), pipeline_mode=pl.Buffered(1)),
                  pl.BlockSpec((d_ff, d), lambda i, t: (0, 0), pipeline_mode=pl.Buffered(1))],
        out_specs=pl.BlockSpec((1, tile, d), lambda i, t: (i, t, 0)),
        out_shape=jax.ShapeDtypeStruct((b, s, d), jnp.float32),
        compiler_params=pltpu.CompilerParams(vmem_limit_bytes=VMEM_LIMIT),
        name="ffn",
    )(x, mod3, g_pre, g_post, w_up, w_down)


def _rope_tables(n_tok):
    t = np.arange(n_tok)
    pos = np.stack([t // GRID_W, t % GRID_W], axis=-1).astype(np.float64)
    inv = ROPE_THETA ** (-np.arange(ROPE_FREQS, dtype=np.float64) / ROPE_FREQS)
    ang = pos[:, :, None] * inv
    cos, sin = np.cos(ang), np.sin(ang)
    cos_h = np.concatenate([cos[:, 0], cos[:, 0], cos[:, 1], cos[:, 1]], axis=-1)
    sin_h = np.concatenate([-sin[:, 0], sin[:, 0], -sin[:, 1], sin[:, 1]], axis=-1)
    cos_p, sin_p = np.tile(cos_h, (1, HEAD_PAIR)), np.tile(sin_h, (1, HEAD_PAIR))
    q_scale = HEAD_DIM ** -0.5 * LOG2E
    return jnp.asarray(np.stack([cos_p * q_scale, sin_p * q_scale, cos_p, sin_p]).astype(np.float32))


def kernel(x, c, ctx, c_ctx, w_ada, b_ada, g_pre_mix, g_post_mix, g_pre_ffn, g_post_ffn, w_in, b_gate, rpb,
           pool_w, pool_scale, w_proj_a, w_proj_b, w_out, w_up, w_down):
    b, s, d = x.shape
    depth = w_ada.shape[0]
    assert depth == 1, "context stream updates are only needed for depth > 1"
    rows = s // GRID_W
    a = ATTN_WIDTH

    c8 = jnp.zeros((8, d), jnp.float32).at[:b].set(c).at[b].set(c_ctx)
    mod = _mod_call(c8, w_ada[0], b_ada[0])
    mod3 = mod[:b].reshape(b, 1, N_MOD * d)
    mod_ctx = mod[b:b + 1, :2 * d].reshape(1, 1, 2 * d)

    w_in_b = _bf16(w_in[0])
    g_pre = g_pre_mix[0].reshape(1, d)
    k_c, v_c = _ctx_kv_call(ctx, mod_ctx, g_pre, w_in_b)
    q, k, v, pooled = _proj_call(x, mod3, g_pre, w_in_b, _rope_tables(s), tile=PROJ_TILE)
    bias = _attn_bias_table(rpb[0], rows)
    attn = _attn_call(q, k, v, k_c, v_c, bias).reshape(b, s, a)
    x1 = _merge_call(x, mod3, g_pre, g_post_mix[0].reshape(1, d), attn, pooled, w_in_b, b_gate[0].reshape(1, -1),
                     _bf16(pool_w[0]), pool_scale[0].reshape(1, -1), _bf16(w_proj_a[0]), _bf16(w_proj_b[0]),
                     _bf16(w_out[0]), tile=2 * SUB_ROWS)
    return _ffn_call(x1, mod3, g_pre_ffn[0].reshape(1, d), g_post_ffn[0].reshape(1, d),
                     _bf16(w_up[0]), _bf16(w_down[0]), tile=2 * SUB_ROWS)
```

```python
import functools

import jax
import jax.numpy as jnp
import numpy as np
from jax.experimental import pallas as pl
from jax.experimental.pallas import tpu as pltpu

D_MODEL = 1024
GRID_W = 64
N_HEADS = 8
HEAD_DIM = 64
ATTN_WIDTH = N_HEADS * HEAD_DIM
POOL_WINDOWS = (2, 4, 8, 16)
N_POOL_GROUPS = len(POOL_WINDOWS)
POOL_WIDTH = 512
POOL_GROUP = POOL_WIDTH // N_POOL_GROUPS
GATE_WIDTH = 2 * D_MODEL
WIN_ROWS = 8
WIN_COLS = 16
Q_COL_BLOCK = 16
K_COL_BLOCK = Q_COL_BLOCK + WIN_COLS
ROPE_FREQS = HEAD_DIM // 4
ROPE_THETA = 10000.0
N_MOD = 6
EPS = 1e-6
NEG_INF = -1e30

LANES = 128
BF16_SUBLANES = 16
LOG2E = 1.4426950408889634
HEAD_PAIR = LANES // HEAD_DIM
N_HEAD_PAIRS = N_HEADS // HEAD_PAIR
POOL_HALO = max(POOL_WINDOWS) // 2
Q_ROWS = 8
BAND_ROWS = 16
KEY_CHUNK = 256
VAL_CHUNK = 256
GATE_CHUNK = 256
PROJ_TILE = 1024
PROJ_SUB_ROWS = 512
SUB_ROWS = 512
MERGE_LAG = 4
VALUE_LAG = 2
N_COL_BLOCKS = GRID_W // Q_COL_BLOCK
VMEM_LIMIT = 56 * 1024 * 1024


def _bf16(a):
    return a.astype(jnp.bfloat16)


def _dot(a, b):
    return jnp.dot(a, b, preferred_element_type=jnp.float32)


def _rms(xf, g):
    return (xf * jax.lax.rsqrt(jnp.mean(xf * xf, axis=-1, keepdims=True) + EPS)) * g


def _sigmoid(z):
    return 1.0 / (1.0 + jnp.exp2(z * -LOG2E))


def _mod_kernel(c_ref, w_ref, b_ref, o_ref):
    c = c_ref[...]
    s = c * _sigmoid(c)
    o_ref[...] = _dot(_bf16(s), _bf16(w_ref[...])) + b_ref[...]


def _mod_call(c8, w_ada, b_ada):
    n = w_ada.shape[1]
    bn = D_MODEL
    return pl.pallas_call(
        _mod_kernel,
        grid=(n // bn,),
        in_specs=[pl.BlockSpec((8, D_MODEL), lambda j: (0, 0)),
                  pl.BlockSpec((D_MODEL, bn), lambda j: (0, j)),
                  pl.BlockSpec((1, bn), lambda j: (0, j))],
        out_specs=pl.BlockSpec((8, bn), lambda j: (0, j)),
        out_shape=jax.ShapeDtypeStruct((8, n), jnp.float32),
        name="mod",
    )(c8, w_ada, b_ada.reshape(1, n))


def _ctx_kv_kernel(ctx_ref, mod_ref, g_ref, wk_ref, wv_ref, k_ref, v_ref):
    mod = mod_ref[0]
    gain = g_ref[...] * (1.0 + mod[:, D_MODEL:2 * D_MODEL])
    h = _bf16(_rms(ctx_ref[0], gain) + mod[:, :D_MODEL])
    k_ref[0] = _bf16(_dot(h, wk_ref[...]))
    v_ref[0] = _bf16(_dot(h, wv_ref[...]))


def _ctx_kv_call(ctx, mod_ctx, g_pre, w_in):
    b, l, d = ctx.shape
    out = jax.ShapeDtypeStruct((b, l, ATTN_WIDTH), jnp.bfloat16)
    return pl.pallas_call(
        _ctx_kv_kernel,
        grid=(b,),
        in_specs=[pl.BlockSpec((1, l, d), lambda i: (i, 0, 0)),
                  pl.BlockSpec((1, 1, 2 * d), lambda i: (0, 0, 0)),
                  pl.BlockSpec((1, d), lambda i: (0, 0)),
                  pl.BlockSpec((d, ATTN_WIDTH), lambda i: (0, 1)),
                  pl.BlockSpec((d, ATTN_WIDTH), lambda i: (0, 2))],
        out_specs=[pl.BlockSpec((1, l, ATTN_WIDTH), lambda i: (i, 0, 0))] * 2,
        out_shape=[out, out],
        name="ctx_kv",
    )(ctx, mod_ctx, g_pre, w_in, w_in)


def _rope(a, cos, sin_signed, first_half):
    swapped = jnp.where(first_half, pltpu.roll(a, LANES - ROPE_FREQS, 1), pltpu.roll(a, ROPE_FREQS, 1))
    return a * cos + swapped * sin_signed


def _window_means_minus_token(ext, tok0, seq):
    halo = POOL_HALO
    n_ext = ext.shape[0]
    n_tok = n_ext - 2 * halo
    tok_head = tok0 + jax.lax.broadcasted_iota(jnp.int32, (halo, POOL_GROUP), 0)
    tok_tail = tok_head + (n_tok - halo)
    pooled = []
    for g, w in enumerate(POOL_WINDOWS):
        e = ext[:, g * POOL_GROUP:(g + 1) * POOL_GROUP]
        acc = e
        span = 1
        while 2 * span < w:
            acc = acc + pltpu.roll(acc, n_ext - span, 0)
            span *= 2
        acc = acc + pltpu.roll(acc, span, 0)

        def window_count(tok):
            return (jnp.minimum(tok + (w - w // 2), seq) - jnp.maximum(tok - w // 2, 0)).astype(jnp.float32)

        mean = jnp.concatenate([acc[halo:2 * halo] / window_count(tok_head),
                                acc[2 * halo:n_tok] * (1.0 / w),
                                acc[n_tok:halo + n_tok] / window_count(tok_tail)], axis=0)
        pooled.append(mean - e[halo:halo + n_tok])
    return jnp.concatenate(pooled, axis=-1)


def _proj_kernel(x_ref, xprev_ref, xnext_ref, mod_ref, g_ref, w_ref, rope_ref,
                 q_ref, k_ref, v_ref, p_ref, *, seq):
    t = pl.program_id(1)
    tile = x_ref.shape[1]
    n_sub = tile // PROJ_SUB_ROWS
    halo = POOL_HALO
    mod = mod_ref[0]
    gain = g_ref[...] * (1.0 + mod[:, D_MODEL:2 * D_MODEL])
    shift = mod[:, :D_MODEL]
    lane = jax.lax.broadcasted_iota(jnp.int32, (PROJ_SUB_ROWS, LANES), 1)
    first_half = (lane % (2 * ROPE_FREQS)) < ROPE_FREQS
    hs = {}

    def pool_part(k):
        r = k * PROJ_SUB_ROWS
        hs[k] = _bf16(_rms(x_ref[0, r:r + PROJ_SUB_ROWS, :], gain) + shift)
        x_prev = xprev_ref[0, 0] if k == 0 else x_ref[0, r - halo:r, :]
        x_next = xnext_ref[0, 0] if k == n_sub - 1 else x_ref[0, r + PROJ_SUB_ROWS:r + PROJ_SUB_ROWS + halo, :]
        h_halo = _bf16(_rms(jnp.concatenate([x_prev, x_next], axis=0), gain) + shift)
        p_all = _dot(jnp.concatenate([hs[k], h_halo], axis=0), w_ref[:, 3 * ATTN_WIDTH:])
        p_prev = p_all[PROJ_SUB_ROWS:PROJ_SUB_ROWS + halo]
        p_next = p_all[PROJ_SUB_ROWS + halo:]
        if k == 0:
            p_prev = jnp.where(t == 0, 0.0, p_prev)
        if k == n_sub - 1:
            p_next = jnp.where(t == pl.num_programs(1) - 1, 0.0, p_next)
        ext = jnp.concatenate([p_prev, p_all[:PROJ_SUB_ROWS], p_next], axis=0)
        p_ref[0, r:r + PROJ_SUB_ROWS, :] = _bf16(_window_means_minus_token(ext, t * tile + r, seq))

    def qkv_part(k):
        r = k * PROJ_SUB_ROWS
        rows = slice(r, r + PROJ_SUB_ROWS)
        proj = _dot(hs.pop(k), w_ref[:, :3 * ATTN_WIDTH])
        cos_q, sin_q, cos_k, sin_k = (rope_ref[i, rows, :] for i in range(4))
        for j in range(N_HEAD_PAIRS):
            sl = slice(j * LANES, (j + 1) * LANES)
            q_ref[0, rows, sl] = _bf16(_rope(proj[:, sl], cos_q, sin_q, first_half))
            ksl = slice(ATTN_WIDTH + j * LANES, ATTN_WIDTH + (j + 1) * LANES)
            k_ref[0, rows, sl] = _bf16(_rope(proj[:, ksl], cos_k, sin_k, first_half))
        v_ref[0, rows, :] = _bf16(proj[:, 2 * ATTN_WIDTH:3 * ATTN_WIDTH])

    for k in range(n_sub):
        pool_part(k)
        qkv_part(k)


def _proj_call(x, mod3, g_pre, w_in, rope, tile):
    b, s, d = x.shape
    n = 3 * ATTN_WIDTH + POOL_WIDTH
    halo = POOL_HALO
    assert POOL_WIDTH == ATTN_WIDTH and 2 * halo == BF16_SUBLANES and tile % BF16_SUBLANES == 0
    x4 = x.reshape(b, s // halo, halo, d)
    n_halo = s // halo
    per = tile // halo
    act = jax.ShapeDtypeStruct((b, s, ATTN_WIDTH), jnp.bfloat16)
    act_spec = pl.BlockSpec((1, tile, ATTN_WIDTH), lambda i, t: (i, t, 0))
    return pl.pallas_call(
        functools.partial(_proj_kernel, seq=s),
        grid=(b, s // tile),
        in_specs=[pl.BlockSpec((1, tile, d), lambda i, t: (i, t, 0)),
                  pl.BlockSpec((1, 1, halo, d), lambda i, t: (i, jnp.maximum(t * per - 1, 0), 0, 0)),
                  pl.BlockSpec((1, 1, halo, d), lambda i, t: (i, jnp.minimum((t + 1) * per, n_halo - 1), 0, 0)),
                  pl.BlockSpec((1, 1, N_MOD * d), lambda i, t: (i, 0, 0)),
                  pl.BlockSpec((1, d), lambda i, t: (0, 0)),
                  pl.BlockSpec((d, n), lambda i, t: (0, 0)),
                  pl.BlockSpec((4, tile, LANES), lambda i, t: (0, t, 0))],
        out_specs=[act_spec] * 4,
        out_shape=[act] * 4,
        compiler_params=pltpu.CompilerParams(vmem_limit_bytes=VMEM_LIMIT),
        name="proj",
    )(x, x4, x4, mod3, g_pre, w_in, rope)


def _band_start(i, rows):
    return jnp.clip(i * Q_ROWS - WIN_ROWS // 2, 0, rows - BAND_ROWS)


def _key_col0(n):
    return min(max(n * Q_COL_BLOCK - WIN_COLS // 2, 0), GRID_W - K_COL_BLOCK)


def _attn_kernel(q_ref, k_ref, v_ref, kc_ref, vc_ref, bias_ref, o_ref, *, rows):
    i = pl.program_id(1)
    start = pl.multiple_of(_band_start(i, rows) * GRID_W, GRID_W)
    n_q = Q_ROWS * Q_COL_BLOCK
    n_k = BAND_ROWS * K_COL_BLOCK
    n_l = HEAD_PAIR * n_q
    n_ctx = kc_ref.shape[1]
    first_d = jax.lax.broadcasted_iota(jnp.int32, (LANES, n_q), 0) < HEAD_DIM
    zero = jnp.zeros((), jnp.bfloat16)
    ones_rows = jnp.ones((BF16_SUBLANES, VAL_CHUNK), jnp.bfloat16)
    per_val = VAL_CHUNK // KEY_CHUNK
    n_key_chunks = (n_k + n_ctx) // KEY_CHUNK
    units = [(hp, n) for hp in range(N_HEAD_PAIRS) for n in range(N_COL_BLOCKS)]
    state = [dict() for _ in units]

    def band_tile(ref, hp, n):
        c0 = _key_col0(n)
        band = ref[0, pl.ds(start, BAND_ROWS * GRID_W), hp * LANES:(hp + 1) * LANES]
        return band.reshape(BAND_ROWS, GRID_W, LANES)[:, c0:c0 + K_COL_BLOCK, :].reshape(n_k, LANES)

    def score_chunk(u, j):
        hp, n = units[u]
        st = state[u]
        if j == 0:
            q_t = q_ref[0, :, hp * LANES:(hp + 1) * LANES].reshape(Q_ROWS, GRID_W, LANES)
            q_t = q_t[:, n * Q_COL_BLOCK:(n + 1) * Q_COL_BLOCK, :].reshape(n_q, LANES).T
            st["q"] = jnp.concatenate([jnp.where(first_d, q_t, zero), jnp.where(first_d, zero, q_t)], axis=1)
            st["k"] = band_tile(k_ref, hp, n)
            st["s"] = []
        lo = j * KEY_CHUNK
        if lo < n_k:
            s_j = _dot(st["k"][lo:lo + KEY_CHUNK], st["q"]) + bias_ref[0, hp, n, lo:lo + KEY_CHUNK, :]
        else:
            s_j = _dot(kc_ref[0, lo - n_k:lo - n_k + KEY_CHUNK, hp * LANES:(hp + 1) * LANES], st["q"])
        st["s"].append(s_j)
        m_j = jnp.max(s_j.reshape(KEY_CHUNK // 8, 8, n_l), axis=0)
        st["m"] = m_j if j == 0 else jnp.maximum(st["m"], m_j)

    def value_chunk(u, jv):
        hp, n = units[u]
        st = state[u]
        if jv == 0:
            st["m"] = jnp.max(st["m"], axis=0, keepdims=True)
            st["v"] = band_tile(v_ref, hp, n)
        p_v = [_bf16(jnp.exp2(st["s"][j] - st["m"])) for j in range(jv * per_val, (jv + 1) * per_val)]
        lo = jv * VAL_CHUNK
        if lo < n_k:
            v_j = st["v"][lo:lo + VAL_CHUNK]
        else:
            v_j = vc_ref[0, lo - n_k:lo - n_k + VAL_CHUNK, hp * LANES:(hp + 1) * LANES]
        v_ext = jnp.concatenate([v_j.T, ones_rows], axis=0)
        o_j = _dot(v_ext, jnp.concatenate(p_v, axis=0))
        st["o"] = o_j if jv == 0 else st["o"] + o_j

    def finish(u):
        hp, n = units[u]
        st = state[u]
        o_t = st["o"]
        inv_l = 1.0 / o_t[LANES:LANES + 1, :]
        o_t = jnp.concatenate([o_t[:HEAD_DIM, :n_q] * inv_l[:, :n_q],
                               o_t[HEAD_DIM:LANES, n_q:] * inv_l[:, n_q:]], axis=0)
        o_ref[0, :, n * Q_COL_BLOCK:(n + 1) * Q_COL_BLOCK, hp * LANES:(hp + 1) * LANES] = (
            _bf16(o_t.T).reshape(Q_ROWS, Q_COL_BLOCK, LANES))
        st.clear()

    n_val_chunks = n_key_chunks // per_val
    n_slots = len(units) * n_val_chunks
    for slot in range(n_slots + n_val_chunks + VALUE_LAG):
        if slot < n_slots:
            u, jv = divmod(slot, n_val_chunks)
            for j in range(jv * per_val, (jv + 1) * per_val):
                score_chunk(u, j)
        done = slot - n_val_chunks - VALUE_LAG
        if done >= 0:
            u, jv = divmod(done, n_val_chunks)
            value_chunk(u, jv)
            if jv == n_val_chunks - 1:
                finish(u)


def _attn_call(q, k, v, k_c, v_c, bias):
    b, s, _ = q.shape
    rows = s // GRID_W
    n_i = rows // Q_ROWS
    l = k_c.shape[1]
    tok = Q_ROWS * GRID_W

    def row_class(i):
        return jnp.where(i == 0, 0, jnp.where(i == n_i - 1, 2, 1))

    assert (BAND_ROWS * K_COL_BLOCK) % VAL_CHUNK == 0 and l % VAL_CHUNK == 0 and VAL_CHUNK % KEY_CHUNK == 0
    whole_seq = pl.Buffered(2)
    return pl.pallas_call(
        functools.partial(_attn_kernel, rows=rows),
        grid=(b, n_i),
        in_specs=[pl.BlockSpec((1, tok, ATTN_WIDTH), lambda bi, i: (bi, i, 0)),
                  pl.BlockSpec((1, s, ATTN_WIDTH), lambda bi, i: (bi, 0, 0), pipeline_mode=whole_seq),
                  pl.BlockSpec((1, s, ATTN_WIDTH), lambda bi, i: (bi, 0, 0), pipeline_mode=whole_seq),
                  pl.BlockSpec((1, l, ATTN_WIDTH), lambda bi, i: (bi, 0, 0)),
                  pl.BlockSpec((1, l, ATTN_WIDTH), lambda bi, i: (bi, 0, 0)),
                  pl.BlockSpec((1, N_HEAD_PAIRS, N_COL_BLOCKS, BAND_ROWS * K_COL_BLOCK,
                                HEAD_PAIR * Q_ROWS * Q_COL_BLOCK),
                               lambda bi, i: (row_class(i), 0, 0, 0, 0))],
        out_specs=pl.BlockSpec((1, Q_ROWS, GRID_W, ATTN_WIDTH), lambda bi, i: (bi, i, 0, 0)),
        out_shape=jax.ShapeDtypeStruct((b, rows, GRID_W, ATTN_WIDTH), jnp.bfloat16),
        compiler_params=pltpu.CompilerParams(vmem_limit_bytes=VMEM_LIMIT),
        name="attn",
    )(q, k, v, k_c, v_c, bias)


N_RPB_ROWS = 2 * WIN_ROWS - 1
N_RPB_COLS = 2 * WIN_COLS - 1
ROW_CLASS_STEPS = (0, 1, -1)


def _row_class_geometry(i, rows):
    start = min(max(i * Q_ROWS - WIN_ROWS // 2, 0), rows - BAND_ROWS)
    r0 = [min(max(i * Q_ROWS + a - WIN_ROWS // 2, 0), rows - WIN_ROWS) - start for a in range(Q_ROWS)]
    return start, start - i * Q_ROWS + WIN_ROWS - 1, r0


def _bias_kernel(rpb_ref, o_ref, toep_ref, *, rows):
    hp = pl.program_id(0)
    n = pl.program_id(1)
    c0 = jnp.clip(n * Q_COL_BLOCK - WIN_COLS // 2, 0, GRID_W - K_COL_BLOCK)
    kc = c0 + jax.lax.broadcasted_iota(jnp.int32, (K_COL_BLOCK, LANES), 0)
    lane = jax.lax.broadcasted_iota(jnp.int32, (K_COL_BLOCK, LANES), 1)
    q_row = lane // Q_COL_BLOCK
    qc = n * Q_COL_BLOCK + lane % Q_COL_BLOCK
    q_col0 = jnp.clip(qc - WIN_COLS // 2, 0, GRID_W - WIN_COLS)
    col_valid = (kc >= q_col0) & (kc < q_col0 + WIN_COLS)
    cidx = jnp.clip(kc - qc + WIN_COLS - 1, 0, N_RPB_COLS - 1)
    masked = jnp.full((K_COL_BLOCK, LANES), NEG_INF, jnp.float32)
    n_i = rows // Q_ROWS
    for e in range(HEAD_PAIR):
        base = (hp * HEAD_PAIR + e) * N_RPB_ROWS
        for ri in range(N_RPB_ROWS):
            rpb_row = jnp.broadcast_to(rpb_ref[pl.ds(base + ri, 1), :], (K_COL_BLOCK, LANES))
            picked = jnp.take_along_axis(rpb_row, cidx, axis=1)
            toep_ref[ri] = jnp.where(col_valid, picked * LOG2E, masked)
        for rc, step in enumerate(ROW_CLASS_STEPS):
            _, ridx0, r0 = _row_class_geometry(step % n_i, rows)
            for ar in range(BAND_ROWS):
                slab = masked
                for a in range(Q_ROWS):
                    if 0 <= ar - r0[a] < WIN_ROWS:
                        slab = jnp.where(q_row == a, toep_ref[ar - a + ridx0], slab)
                o_ref[rc, 0, 0, ar * K_COL_BLOCK:(ar + 1) * K_COL_BLOCK, e * LANES:(e + 1) * LANES] = slab


def _attn_bias_table(rpb, rows):
    assert rows // Q_ROWS >= 3 and rows % Q_ROWS == 0
    assert Q_ROWS * Q_COL_BLOCK == LANES
    n_q, n_k = HEAD_PAIR * Q_ROWS * Q_COL_BLOCK, BAND_ROWS * K_COL_BLOCK
    n_rc = len(ROW_CLASS_STEPS)
    return pl.pallas_call(
        functools.partial(_bias_kernel, rows=rows),
        grid=(N_HEAD_PAIRS, N_COL_BLOCKS),
        in_specs=[pl.BlockSpec((N_HEADS * N_RPB_ROWS, LANES), lambda hp, n: (0, 0))],
        out_specs=pl.BlockSpec((n_rc, 1, 1, n_k, n_q), lambda hp, n: (0, hp, n, 0, 0)),
        out_shape=jax.ShapeDtypeStruct((n_rc, N_HEAD_PAIRS, N_COL_BLOCKS, n_k, n_q), jnp.float32),
        scratch_shapes=[pltpu.VMEM((N_RPB_ROWS, K_COL_BLOCK, LANES), jnp.float32)],
        name="bias",
    )(jnp.pad(rpb.reshape(N_HEADS * N_RPB_ROWS, N_RPB_COLS), ((0, 0), (0, LANES - N_RPB_COLS))))


def _pool_fold_kernel(pw_ref, ps_ref, wb_ref, o_ref):
    o_ref[...] = _bf16(jnp.dot(pw_ref[0] * ps_ref[0], wb_ref[...], precision=jax.lax.Precision.HIGHEST,
                               preferred_element_type=jnp.float32))


def _pool_fold_call(pool_w, pool_scale, w_proj_b):
    d = w_proj_b.shape[1]
    return pl.pallas_call(
        _pool_fold_kernel,
        grid=(N_POOL_GROUPS,),
        in_specs=[pl.BlockSpec((1, POOL_GROUP, POOL_GROUP), lambda g: (g, 0, 0)),
                  pl.BlockSpec((1, 1, POOL_GROUP), lambda g: (g, 0, 0)),
                  pl.BlockSpec((POOL_GROUP, d), lambda g: (g, 0))],
        out_specs=pl.BlockSpec((POOL_GROUP, d), lambda g: (g, 0)),
        out_shape=jax.ShapeDtypeStruct((POOL_WIDTH, d), jnp.bfloat16),
        name="pool_fold",
    )(pool_w, pool_scale.reshape(N_POOL_GROUPS, 1, POOL_GROUP), w_proj_b)


def _merge_kernel(x_ref, mod_ref, gpre_ref, gpost_ref, attn_ref, p_ref,
                  wg_ref, bg_ref, wa_ref, wb_ref, wo_ref, o_ref):
    n_sub = x_ref.shape[1] // SUB_ROWS
    mod = mod_ref[0]
    g_in = gpre_ref[...] * (1.0 + mod[:, D_MODEL:2 * D_MODEL])
    g_out = gpost_ref[...] * mod[:, 2 * D_MODEL:3 * D_MODEL]
    n_chunks = D_MODEL // GATE_CHUNK
    lead = min(3, n_chunks)
    state = [dict() for _ in range(n_sub)]

    def gate_pair(st, c):
        lo = c * GATE_CHUNK
        g_a = _sigmoid(_dot(st["h"], wg_ref[:, lo:lo + GATE_CHUNK]) + bg_ref[:, lo:lo + GATE_CHUNK])
        lo += D_MODEL
        g_b = _sigmoid(_dot(st["h"], wg_ref[:, lo:lo + GATE_CHUNK]) + bg_ref[:, lo:lo + GATE_CHUNK])
        st["gates"].append((g_a, g_b))

    def out_partial(st, c):
        g_a, g_b = st["gates"][c]
        sl = slice(c * GATE_CHUNK, (c + 1) * GATE_CHUNK)
        part = _dot(_bf16(g_a * st["ya"][:, sl] + g_b * st["yb"][:, sl]), wo_ref[sl, :])
        st["y"] = part if c == 0 else st["y"] + part

    def stage(k, s):
        st = state[k]
        rows = slice(k * SUB_ROWS, (k + 1) * SUB_ROWS)
        if s == 0:
            st["ya"] = _dot(attn_ref[0, rows, :], wa_ref[...])
            st["yb"] = _dot(p_ref[0, rows, :], wb_ref[...])
        elif s <= lead:
            if s == 1:
                st["h"] = _bf16(_rms(x_ref[0, rows, :], g_in) + mod[:, :D_MODEL])
                st["gates"] = []
            gate_pair(st, s - 1)
        else:
            c = s - lead - 1
            if c + lead < n_chunks:
                gate_pair(st, c + lead)
            out_partial(st, c)
            if c == n_chunks - 1:
                o_ref[0, rows, :] = x_ref[0, rows, :] + _rms(st["y"], g_out)
                st.clear()

    n_stages = lead + 1 + n_chunks
    for step in range(n_stages + MERGE_LAG * (n_sub - 1)):
        for k in range(n_sub):
            s = step - k * MERGE_LAG
            if 0 <= s < n_stages:
                stage(k, s)


def _merge_call(x, mod3, g_pre, g_post, attn, pooled, w_in, b_gate, w_a, w_b, w_o, tile):
    b, s, d = x.shape
    assert w_in.shape[1] == 2 * GATE_WIDTH

    def const(shape):
        return pl.BlockSpec(shape, lambda i, t: (0,) * len(shape))

    return pl.pallas_call(
        _merge_kernel,
        grid=(b, s // tile),
        in_specs=[pl.BlockSpec((1, tile, d), lambda i, t: (i, t, 0)),
                  pl.BlockSpec((1, 1, N_MOD * d), lambda i, t: (i, 0, 0)),
                  const((1, d)), const((1, d)),
                  pl.BlockSpec((1, tile, ATTN_WIDTH), lambda i, t: (i, t, 0)),
                  pl.BlockSpec((1, tile, POOL_WIDTH), lambda i, t: (i, t, 0)),
                  pl.BlockSpec((d, GATE_WIDTH), lambda i, t: (0, 1)), const((1, GATE_WIDTH)),
                  const((ATTN_WIDTH, d)), const((POOL_WIDTH, d)), const((d, d))],
        out_specs=pl.BlockSpec((1, tile, d), lambda i, t: (i, t, 0)),
        out_shape=jax.ShapeDtypeStruct((b, s, d), jnp.float32),
        compiler_params=pltpu.CompilerParams(vmem_limit_bytes=VMEM_LIMIT),
        name="merge",
    )(x, mod3, g_pre, g_post, attn, pooled, w_in, b_gate, w_a, w_b, w_o)


def _ffn_kernel(x_ref, mod_ref, gpre_ref, gpost_ref, wu_ref, wd_ref, o_ref, *, d_ff, chunk):
    mod = mod_ref[0]
    g_in = gpre_ref[...] * (1.0 + mod[:, 4 * D_MODEL:5 * D_MODEL])
    g_out = gpost_ref[...] * mod[:, 5 * D_MODEL:]
    subs = list(range(0, x_ref.shape[1], SUB_ROWS))
    n_chunks = d_ff // chunk
    lag = n_chunks // 2
    hs, accs = {}, {}

    def ffn_chunk(r, c):
        if c == 0:
            hs[r] = _bf16(_rms(x_ref[0, r:r + SUB_ROWS, :], g_in) + mod[:, 3 * D_MODEL:4 * D_MODEL])
        c0 = c * chunk
        gate = _dot(hs[r], wu_ref[:, c0:c0 + chunk])
        up = _dot(hs[r], wu_ref[:, d_ff + c0:d_ff + c0 + chunk])
        part = _dot(_bf16(gate * _sigmoid(gate) * up), wd_ref[c0:c0 + chunk, :])
        accs[r] = part if c == 0 else accs[r] + part
        if c == n_chunks - 1:
            o_ref[0, r:r + SUB_ROWS, :] = x_ref[0, r:r + SUB_ROWS, :] + _rms(accs[r], g_out)

    for step in range(n_chunks + lag * (len(subs) - 1)):
        for idx, r in enumerate(subs):
            c = step - idx * lag
            if 0 <= c < n_chunks:
                ffn_chunk(r, c)


def _ffn_call(x, mod3, g_pre, g_post, w_up, w_down, tile):
    b, s, d = x.shape
    d_ff = w_down.shape[0]
    chunk = 2 * LANES
    assert d_ff % chunk == 0

    def const(shape):
        return pl.BlockSpec(shape, lambda i, t: (0,) * len(shape))

    return pl.pallas_call(
        functools.partial(_ffn_kernel, d_ff=d_ff, chunk=chunk),
        grid=(b, s // tile),
        in_specs=[pl.BlockSpec((1, tile, d), lambda i, t: (i, t, 0)),
                  pl.BlockSpec((1, 1, N_MOD * d), lambda i, t: (i, 0, 0)),
                  const((1, d)), const((1, d)),
                  pl.BlockSpec((d, 2 * d_ff), lambda i, t: (0, 0), pipeline_mode=pl.Buffered(1)),
                  pl.BlockSpec((d_ff, d), lambda i, t: (0, 0), pipeline_mode=pl.Buffered(1))],
        out_specs=pl.BlockSpec((1, tile, d), lambda i, t: (i, t, 0)),
        out_shape=jax.ShapeDtypeStruct((b, s, d), jnp.float32),
        compiler_params=pltpu.CompilerParams(vmem_limit_bytes=VMEM_LIMIT),
        name="ffn",
    )(x, mod3, g_pre, g_post, w_up, w_down)


def _rope_tables(n_tok):
    t = np.arange(n_tok)
    pos = np.stack([t // GRID_W, t % GRID_W], axis=-1).astype(np.float64)
    inv = ROPE_THETA ** (-np.arange(ROPE_FREQS, dtype=np.float64) / ROPE_FREQS)
    ang = pos[:, :, None] * inv
    cos, sin = np.cos(ang), np.sin(ang)
    cos_h = np.concatenate([cos[:, 0], cos[:, 0], cos[:, 1], cos[:, 1]], axis=-1)
    sin_h = np.concatenate([-sin[:, 0], sin[:, 0], -sin[:, 1], sin[:, 1]], axis=-1)
    cos_p, sin_p = np.tile(cos_h, (1, HEAD_PAIR)), np.tile(sin_h, (1, HEAD_PAIR))
    q_scale = HEAD_DIM ** -0.5 * LOG2E
    return jnp.asarray(np.stack([cos_p * q_scale, sin_p * q_scale, cos_p, sin_p]).astype(np.float32))


def kernel(x, c, ctx, c_ctx, w_ada, b_ada, g_pre_mix, g_post_mix, g_pre_ffn, g_post_ffn, w_in, b_gate, rpb,
           pool_w, pool_scale, w_proj_a, w_proj_b, w_out, w_up, w_down):
    b, s, d = x.shape
    depth = w_ada.shape[0]
    assert depth == 1, "context stream updates are only needed for depth > 1"
    rows = s // GRID_W
    a = ATTN_WIDTH

    c8 = jnp.zeros((8, d), jnp.float32).at[:b].set(c).at[b].set(c_ctx)
    mod = _mod_call(c8, w_ada[0], b_ada[0])
    mod3 = mod[:b].reshape(b, 1, N_MOD * d)
    mod_ctx = mod[b:b + 1, :2 * d].reshape(1, 1, 2 * d)

    w_in_b = _bf16(w_in[0])
    g_pre = g_pre_mix[0].reshape(1, d)
    k_c, v_c = _ctx_kv_call(ctx, mod_ctx, g_pre, w_in_b)
    q, k, v, pooled = _proj_call(x, mod3, g_pre, w_in_b, _rope_tables(s), tile=PROJ_TILE)
    bias = _attn_bias_table(rpb[0], rows)
    attn = _attn_call(q, k, v, k_c, v_c, bias).reshape(b, s, a)
    w_pool = _pool_fold_call(pool_w[0], pool_scale[0], w_proj_b[0])
    x1 = _merge_call(x, mod3, g_pre, g_post_mix[0].reshape(1, d), attn, pooled, w_in_b, b_gate[0].reshape(1, -1),
                     _bf16(w_proj_a[0]), w_pool, _bf16(w_out[0]), tile=2 * SUB_ROWS)
    return _ffn_call(x1, mod3, g_pre_ffn[0].reshape(1, d), g_post_ffn[0].reshape(1, d),
                     _bf16(w_up[0]), _bf16(w_down[0]), tile=2 * SUB_ROWS)
```

```python
import functools

import jax
import jax.numpy as jnp
import numpy as np
from jax.experimental import pallas as pl
from jax.experimental.pallas import tpu as pltpu

D_MODEL = 1024
GRID_W = 64
N_HEADS = 8
HEAD_DIM = 64
ATTN_WIDTH = N_HEADS * HEAD_DIM
POOL_WINDOWS = (2, 4, 8, 16)
N_POOL_GROUPS = len(POOL_WINDOWS)
POOL_WIDTH = 512
POOL_GROUP = POOL_WIDTH // N_POOL_GROUPS
GATE_WIDTH = 2 * D_MODEL
WIN_ROWS = 8
WIN_COLS = 16
Q_COL_BLOCK = 16
K_COL_BLOCK = Q_COL_BLOCK + WIN_COLS
ROPE_FREQS = HEAD_DIM // 4
ROPE_THETA = 10000.0
N_MOD = 6
EPS = 1e-6
NEG_INF = -1e30

LANES = 128
BF16_SUBLANES = 16
LOG2E = 1.4426950408889634
HEAD_PAIR = LANES // HEAD_DIM
N_HEAD_PAIRS = N_HEADS // HEAD_PAIR
POOL_HALO = max(POOL_WINDOWS) // 2
Q_ROWS = 8
BAND_ROWS = 16
KEY_CHUNK = 256
VAL_CHUNK = 256
GATE_CHUNK = 256
PROJ_TILE = 1024
PROJ_SUB_ROWS = 1024
SUB_ROWS = 512
MERGE_LAG = 4
VALUE_LAG = 2
N_COL_BLOCKS = GRID_W // Q_COL_BLOCK
VMEM_LIMIT = 56 * 1024 * 1024


def _bf16(a):
    return a.astype(jnp.bfloat16)


def _dot(a, b):
    return jnp.dot(a, b, preferred_element_type=jnp.float32)


def _rms(xf, g):
    return (xf * jax.lax.rsqrt(jnp.mean(xf * xf, axis=-1, keepdims=True) + EPS)) * g


def _sigmoid(z):
    return 1.0 / (1.0 + jnp.exp2(z * -LOG2E))


def _mod_kernel(c_ref, w_ref, b_ref, o_ref):
    c = c_ref[...]
    s = c * _sigmoid(c)
    o_ref[...] = _dot(_bf16(s), _bf16(w_ref[...])) + b_ref[...]


def _mod_call(c8, w_ada, b_ada):
    n = w_ada.shape[1]
    bn = D_MODEL
    return pl.pallas_call(
        _mod_kernel,
        grid=(n // bn,),
        in_specs=[pl.BlockSpec((8, D_MODEL), lambda j: (0, 0)),
                  pl.BlockSpec((D_MODEL, bn), lambda j: (0, j)),
                  pl.BlockSpec((1, bn), lambda j: (0, j))],
        out_specs=pl.BlockSpec((8, bn), lambda j: (0, j)),
        out_shape=jax.ShapeDtypeStruct((8, n), jnp.float32),
        name="mod",
    )(c8, w_ada, b_ada.reshape(1, n))


def _ctx_kv_kernel(ctx_ref, mod_ref, g_ref, wk_ref, wv_ref, k_ref, v_ref):
    mod = mod_ref[0]
    gain = g_ref[...] * (1.0 + mod[:, D_MODEL:2 * D_MODEL])
    h = _bf16(_rms(ctx_ref[0], gain) + mod[:, :D_MODEL])
    k_ref[0] = _bf16(_dot(h, wk_ref[...]))
    v_ref[0] = _bf16(_dot(h, wv_ref[...]))


def _ctx_kv_call(ctx, mod_ctx, g_pre, w_in):
    b, l, d = ctx.shape
    out = jax.ShapeDtypeStruct((b, l, ATTN_WIDTH), jnp.bfloat16)
    return pl.pallas_call(
        _ctx_kv_kernel,
        grid=(b,),
        in_specs=[pl.BlockSpec((1, l, d), lambda i: (i, 0, 0)),
                  pl.BlockSpec((1, 1, 2 * d), lambda i: (0, 0, 0)),
                  pl.BlockSpec((1, d), lambda i: (0, 0)),
                  pl.BlockSpec((d, ATTN_WIDTH), lambda i: (0, 1)),
                  pl.BlockSpec((d, ATTN_WIDTH), lambda i: (0, 2))],
        out_specs=[pl.BlockSpec((1, l, ATTN_WIDTH), lambda i: (i, 0, 0))] * 2,
        out_shape=[out, out],
        name="ctx_kv",
    )(ctx, mod_ctx, g_pre, w_in, w_in)


def _rope(a, cos, sin_signed, first_half):
    swapped = jnp.where(first_half, pltpu.roll(a, LANES - ROPE_FREQS, 1), pltpu.roll(a, ROPE_FREQS, 1))
    return a * cos + swapped * sin_signed


def _window_means_minus_token(ext, tok0, seq):
    halo = POOL_HALO
    n_ext = ext.shape[0]
    n_tok = n_ext - 2 * halo
    tok_head = tok0 + jax.lax.broadcasted_iota(jnp.int32, (halo, POOL_GROUP), 0)
    tok_tail = tok_head + (n_tok - halo)
    pooled = []
    for g, w in enumerate(POOL_WINDOWS):
        e = ext[:, g * POOL_GROUP:(g + 1) * POOL_GROUP]
        acc = e
        span = 1
        while 2 * span < w:
            acc = acc + pltpu.roll(acc, n_ext - span, 0)
            span *= 2
        acc = acc + pltpu.roll(acc, span, 0)

        def window_count(tok):
            return (jnp.minimum(tok + (w - w // 2), seq) - jnp.maximum(tok - w // 2, 0)).astype(jnp.float32)

        mean = jnp.concatenate([acc[halo:2 * halo] / window_count(tok_head),
                                acc[2 * halo:n_tok] * (1.0 / w),
                                acc[n_tok:halo + n_tok] / window_count(tok_tail)], axis=0)
        pooled.append(mean - e[halo:halo + n_tok])
    return jnp.concatenate(pooled, axis=-1)


def _proj_kernel(x_ref, xprev_ref, xnext_ref, mod_ref, g_ref, w_ref, rope_ref,
                 q_ref, k_ref, v_ref, p_ref, *, seq):
    t = pl.program_id(1)
    tile = x_ref.shape[1]
    n_sub = tile // PROJ_SUB_ROWS
    halo = POOL_HALO
    mod = mod_ref[0]
    gain = g_ref[...] * (1.0 + mod[:, D_MODEL:2 * D_MODEL])
    shift = mod[:, :D_MODEL]
    lane = jax.lax.broadcasted_iota(jnp.int32, (PROJ_SUB_ROWS, LANES), 1)
    first_half = (lane % (2 * ROPE_FREQS)) < ROPE_FREQS
    hs = {}

    def pool_part(k):
        r = k * PROJ_SUB_ROWS
        hs[k] = _bf16(_rms(x_ref[0, r:r + PROJ_SUB_ROWS, :], gain) + shift)
        x_prev = xprev_ref[0, 0] if k == 0 else x_ref[0, r - halo:r, :]
        x_next = xnext_ref[0, 0] if k == n_sub - 1 else x_ref[0, r + PROJ_SUB_ROWS:r + PROJ_SUB_ROWS + halo, :]
        h_halo = _bf16(_rms(jnp.concatenate([x_prev, x_next], axis=0), gain) + shift)
        p_all = _dot(jnp.concatenate([hs[k], h_halo], axis=0), w_ref[:, 3 * ATTN_WIDTH:])
        p_prev = p_all[PROJ_SUB_ROWS:PROJ_SUB_ROWS + halo]
        p_next = p_all[PROJ_SUB_ROWS + halo:]
        if k == 0:
            p_prev = jnp.where(t == 0, 0.0, p_prev)
        if k == n_sub - 1:
            p_next = jnp.where(t == pl.num_programs(1) - 1, 0.0, p_next)
        ext = jnp.concatenate([p_prev, p_all[:PROJ_SUB_ROWS], p_next], axis=0)
        p_ref[0, r:r + PROJ_SUB_ROWS, :] = _bf16(_window_means_minus_token(ext, t * tile + r, seq))

    def qkv_part(k):
        r = k * PROJ_SUB_ROWS
        rows = slice(r, r + PROJ_SUB_ROWS)
        proj = _dot(hs.pop(k), w_ref[:, :3 * ATTN_WIDTH])
        cos_q, sin_q, cos_k, sin_k = (rope_ref[i, rows, :] for i in range(4))
        for j in range(N_HEAD_PAIRS):
            sl = slice(j * LANES, (j + 1) * LANES)
            q_ref[0, rows, sl] = _bf16(_rope(proj[:, sl], cos_q, sin_q, first_half))
            ksl = slice(ATTN_WIDTH + j * LANES, ATTN_WIDTH + (j + 1) * LANES)
            k_ref[0, rows, sl] = _bf16(_rope(proj[:, ksl], cos_k, sin_k, first_half))
        v_ref[0, rows, :] = _bf16(proj[:, 2 * ATTN_WIDTH:3 * ATTN_WIDTH])

    for k in range(n_sub):
        pool_part(k)
        qkv_part(k)


def _proj_call(x, mod3, g_pre, w_in, rope, tile):
    b, s, d = x.shape
    n = 3 * ATTN_WIDTH + POOL_WIDTH
    halo = POOL_HALO
    assert POOL_WIDTH == ATTN_WIDTH and 2 * halo == BF16_SUBLANES and tile % BF16_SUBLANES == 0
    x4 = x.reshape(b, s // halo, halo, d)
    n_halo = s // halo
    per = tile // halo
    act = jax.ShapeDtypeStruct((b, s, ATTN_WIDTH), jnp.bfloat16)
    act_spec = pl.BlockSpec((1, tile, ATTN_WIDTH), lambda i, t: (i, t, 0))
    return pl.pallas_call(
        functools.partial(_proj_kernel, seq=s),
        grid=(b, s // tile),
        in_specs=[pl.BlockSpec((1, tile, d), lambda i, t: (i, t, 0)),
                  pl.BlockSpec((1, 1, halo, d), lambda i, t: (i, jnp.maximum(t * per - 1, 0), 0, 0)),
                  pl.BlockSpec((1, 1, halo, d), lambda i, t: (i, jnp.minimum((t + 1) * per, n_halo - 1), 0, 0)),
                  pl.BlockSpec((1, 1, N_MOD * d), lambda i, t: (i, 0, 0)),
                  pl.BlockSpec((1, d), lambda i, t: (0, 0)),
                  pl.BlockSpec((d, n), lambda i, t: (0, 0)),
                  pl.BlockSpec((4, tile, LANES), lambda i, t: (0, t, 0))],
        out_specs=[act_spec] * 4,
        out_shape=[act] * 4,
        compiler_params=pltpu.CompilerParams(vmem_limit_bytes=VMEM_LIMIT),
        name="proj",
    )(x, x4, x4, mod3, g_pre, w_in, rope)


def _band_start(i, rows):
    return jnp.clip(i * Q_ROWS - WIN_ROWS // 2, 0, rows - BAND_ROWS)


def _key_col0(n):
    return min(max(n * Q_COL_BLOCK - WIN_COLS // 2, 0), GRID_W - K_COL_BLOCK)


def _attn_kernel(q_ref, k_ref, v_ref, kc_ref, vc_ref, bias_ref, o_ref, *, rows):
    i = pl.program_id(1)
    start = pl.multiple_of(_band_start(i, rows) * GRID_W, GRID_W)
    n_q = Q_ROWS * Q_COL_BLOCK
    n_k = BAND_ROWS * K_COL_BLOCK
    n_l = HEAD_PAIR * n_q
    n_ctx = kc_ref.shape[1]
    first_d = jax.lax.broadcasted_iota(jnp.int32, (LANES, n_q), 0) < HEAD_DIM
    zero = jnp.zeros((), jnp.bfloat16)
    ones_rows = jnp.ones((BF16_SUBLANES, VAL_CHUNK), jnp.bfloat16)
    per_val = VAL_CHUNK // KEY_CHUNK
    n_key_chunks = (n_k + n_ctx) // KEY_CHUNK
    units = [(hp, n) for hp in range(N_HEAD_PAIRS) for n in range(N_COL_BLOCKS)]
    state = [dict() for _ in units]

    def band_tile(ref, hp, n):
        c0 = _key_col0(n)
        band = ref[0, pl.ds(start, BAND_ROWS * GRID_W), hp * LANES:(hp + 1) * LANES]
        return band.reshape(BAND_ROWS, GRID_W, LANES)[:, c0:c0 + K_COL_BLOCK, :].reshape(n_k, LANES)

    def score_chunk(u, j):
        hp, n = units[u]
        st = state[u]
        if j == 0:
            q_t = q_ref[0, :, hp * LANES:(hp + 1) * LANES].reshape(Q_ROWS, GRID_W, LANES)
            q_t = q_t[:, n * Q_COL_BLOCK:(n + 1) * Q_COL_BLOCK, :].reshape(n_q, LANES).T
            st["q"] = jnp.concatenate([jnp.where(first_d, q_t, zero), jnp.where(first_d, zero, q_t)], axis=1)
            st["k"] = band_tile(k_ref, hp, n)
            st["s"] = []
        lo = j * KEY_CHUNK
        if lo < n_k:
            s_j = _dot(st["k"][lo:lo + KEY_CHUNK], st["q"]) + bias_ref[0, hp, n, lo:lo + KEY_CHUNK, :]
        else:
            s_j = _dot(kc_ref[0, lo - n_k:lo - n_k + KEY_CHUNK, hp * LANES:(hp + 1) * LANES], st["q"])
        st["s"].append(s_j)
        m_j = jnp.max(s_j.reshape(KEY_CHUNK // 8, 8, n_l), axis=0)
        st["m"] = m_j if j == 0 else jnp.maximum(st["m"], m_j)

    def value_chunk(u, jv):
        hp, n = units[u]
        st = state[u]
        if jv == 0:
            st["m"] = jnp.max(st["m"], axis=0, keepdims=True)
            st["v"] = band_tile(v_ref, hp, n)
        p_v = [_bf16(jnp.exp2(st["s"][j] - st["m"])) for j in range(jv * per_val, (jv + 1) * per_val)]
        lo = jv * VAL_CHUNK
        if lo < n_k:
            v_j = st["v"][lo:lo + VAL_CHUNK]
        else:
            v_j = vc_ref[0, lo - n_k:lo - n_k + VAL_CHUNK, hp * LANES:(hp + 1) * LANES]
        v_ext = jnp.concatenate([v_j.T, ones_rows], axis=0)
        o_j = _dot(v_ext, jnp.concatenate(p_v, axis=0))
        st["o"] = o_j if jv == 0 else st["o"] + o_j

    def finish(u):
        hp, n = units[u]
        st = state[u]
        o_t = st["o"]
        inv_l = 1.0 / o_t[LANES:LANES + 1, :]
        o_t = jnp.concatenate([o_t[:HEAD_DIM, :n_q] * inv_l[:, :n_q],
                               o_t[HEAD_DIM:LANES, n_q:] * inv_l[:, n_q:]], axis=0)
        o_ref[0, :, n * Q_COL_BLOCK:(n + 1) * Q_COL_BLOCK, hp * LANES:(hp + 1) * LANES] = (
            _bf16(o_t.T).reshape(Q_ROWS, Q_COL_BLOCK, LANES))
        st.clear()

    n_val_chunks = n_key_chunks // per_val
    n_slots = len(units) * n_val_chunks
    for slot in range(n_slots + n_val_chunks + VALUE_LAG):
        if slot < n_slots:
            u, jv = divmod(slot, n_val_chunks)
            for j in range(jv * per_val, (jv + 1) * per_val):
                score_chunk(u, j)
        done = slot - n_val_chunks - VALUE_LAG
        if done >= 0:
            u, jv = divmod(done, n_val_chunks)
            value_chunk(u, jv)
            if jv == n_val_chunks - 1:
                finish(u)


def _attn_call(q, k, v, k_c, v_c, bias):
    b, s, _ = q.shape
    rows = s // GRID_W
    n_i = rows // Q_ROWS
    l = k_c.shape[1]
    tok = Q_ROWS * GRID_W

    def row_class(i):
        return jnp.where(i == 0, 0, jnp.where(i == n_i - 1, 2, 1))

    assert (BAND_ROWS * K_COL_BLOCK) % VAL_CHUNK == 0 and l % VAL_CHUNK == 0 and VAL_CHUNK % KEY_CHUNK == 0
    whole_seq = pl.Buffered(2)
    return pl.pallas_call(
        functools.partial(_attn_kernel, rows=rows),
        grid=(b, n_i),
        in_specs=[pl.BlockSpec((1, tok, ATTN_WIDTH), lambda bi, i: (bi, i, 0)),
                  pl.BlockSpec((1, s, ATTN_WIDTH), lambda bi, i: (bi, 0, 0), pipeline_mode=whole_seq),
                  pl.BlockSpec((1, s, ATTN_WIDTH), lambda bi, i: (bi, 0, 0), pipeline_mode=whole_seq),
                  pl.BlockSpec((1, l, ATTN_WIDTH), lambda bi, i: (bi, 0, 0)),
                  pl.BlockSpec((1, l, ATTN_WIDTH), lambda bi, i: (bi, 0, 0)),
                  pl.BlockSpec((1, N_HEAD_PAIRS, N_COL_BLOCKS, BAND_ROWS * K_COL_BLOCK,
                                HEAD_PAIR * Q_ROWS * Q_COL_BLOCK),
                               lambda bi, i: (row_class(i), 0, 0, 0, 0))],
        out_specs=pl.BlockSpec((1, Q_ROWS, GRID_W, ATTN_WIDTH), lambda bi, i: (bi, i, 0, 0)),
        out_shape=jax.ShapeDtypeStruct((b, rows, GRID_W, ATTN_WIDTH), jnp.bfloat16),
        compiler_params=pltpu.CompilerParams(vmem_limit_bytes=VMEM_LIMIT),
        name="attn",
    )(q, k, v, k_c, v_c, bias)


N_RPB_ROWS = 2 * WIN_ROWS - 1
N_RPB_COLS = 2 * WIN_COLS - 1
ROW_CLASS_STEPS = (0, 1, -1)


def _row_class_geometry(i, rows):
    start = min(max(i * Q_ROWS - WIN_ROWS // 2, 0), rows - BAND_ROWS)
    r0 = [min(max(i * Q_ROWS + a - WIN_ROWS // 2, 0), rows - WIN_ROWS) - start for a in range(Q_ROWS)]
    return start, start - i * Q_ROWS + WIN_ROWS - 1, r0


def _bias_kernel(rpb_ref, o_ref, toep_ref, *, rows):
    hp = pl.program_id(0)
    n = pl.program_id(1)
    c0 = jnp.clip(n * Q_COL_BLOCK - WIN_COLS // 2, 0, GRID_W - K_COL_BLOCK)
    kc = c0 + jax.lax.broadcasted_iota(jnp.int32, (K_COL_BLOCK, LANES), 0)
    lane = jax.lax.broadcasted_iota(jnp.int32, (K_COL_BLOCK, LANES), 1)
    q_row = lane // Q_COL_BLOCK
    qc = n * Q_COL_BLOCK + lane % Q_COL_BLOCK
    q_col0 = jnp.clip(qc - WIN_COLS // 2, 0, GRID_W - WIN_COLS)
    col_valid = (kc >= q_col0) & (kc < q_col0 + WIN_COLS)
    cidx = jnp.clip(kc - qc + WIN_COLS - 1, 0, N_RPB_COLS - 1)
    masked = jnp.full((K_COL_BLOCK, LANES), NEG_INF, jnp.float32)
    n_i = rows // Q_ROWS
    for e in range(HEAD_PAIR):
        base = (hp * HEAD_PAIR + e) * N_RPB_ROWS
        for ri in range(N_RPB_ROWS):
            rpb_row = jnp.broadcast_to(rpb_ref[pl.ds(base + ri, 1), :], (K_COL_BLOCK, LANES))
            picked = jnp.take_along_axis(rpb_row, cidx, axis=1)
            toep_ref[ri] = jnp.where(col_valid, picked * LOG2E, masked)
        for rc, step in enumerate(ROW_CLASS_STEPS):
            _, ridx0, r0 = _row_class_geometry(step % n_i, rows)
            for ar in range(BAND_ROWS):
                slab = masked
                for a in range(Q_ROWS):
                    if 0 <= ar - r0[a] < WIN_ROWS:
                        slab = jnp.where(q_row == a, toep_ref[ar - a + ridx0], slab)
                o_ref[rc, 0, 0, ar * K_COL_BLOCK:(ar + 1) * K_COL_BLOCK, e * LANES:(e + 1) * LANES] = slab


def _attn_bias_table(rpb, rows):
    assert rows // Q_ROWS >= 3 and rows % Q_ROWS == 0
    assert Q_ROWS * Q_COL_BLOCK == LANES
    n_q, n_k = HEAD_PAIR * Q_ROWS * Q_COL_BLOCK, BAND_ROWS * K_COL_BLOCK
    n_rc = len(ROW_CLASS_STEPS)
    return pl.pallas_call(
        functools.partial(_bias_kernel, rows=rows),
        grid=(N_HEAD_PAIRS, N_COL_BLOCKS),
        in_specs=[pl.BlockSpec((N_HEADS * N_RPB_ROWS, LANES), lambda hp, n: (0, 0))],
        out_specs=pl.BlockSpec((n_rc, 1, 1, n_k, n_q), lambda hp, n: (0, hp, n, 0, 0)),
        out_shape=jax.ShapeDtypeStruct((n_rc, N_HEAD_PAIRS, N_COL_BLOCKS, n_k, n_q), jnp.float32),
        scratch_shapes=[pltpu.VMEM((N_RPB_ROWS, K_COL_BLOCK, LANES), jnp.float32)],
        name="bias",
    )(jnp.pad(rpb.reshape(N_HEADS * N_RPB_ROWS, N_RPB_COLS), ((0, 0), (0, LANES - N_RPB_COLS))))


def _pool_fold_kernel(pw_ref, ps_ref, wb_ref, o_ref):
    o_ref[...] = _bf16(jnp.dot(pw_ref[0] * ps_ref[0], wb_ref[...], precision=jax.lax.Precision.HIGHEST,
                               preferred_element_type=jnp.float32))


def _pool_fold_call(pool_w, pool_scale, w_proj_b):
    d = w_proj_b.shape[1]
    return pl.pallas_call(
        _pool_fold_kernel,
        grid=(N_POOL_GROUPS,),
        in_specs=[pl.BlockSpec((1, POOL_GROUP, POOL_GROUP), lambda g: (g, 0, 0)),
                  pl.BlockSpec((1, 1, POOL_GROUP), lambda g: (g, 0, 0)),
                  pl.BlockSpec((POOL_GROUP, d), lambda g: (g, 0))],
        out_specs=pl.BlockSpec((POOL_GROUP, d), lambda g: (g, 0)),
        out_shape=jax.ShapeDtypeStruct((POOL_WIDTH, d), jnp.bfloat16),
        name="pool_fold",
    )(pool_w, pool_scale.reshape(N_POOL_GROUPS, 1, POOL_GROUP), w_proj_b)


def _merge_kernel(x_ref, mod_ref, gpre_ref, gpost_ref, attn_ref, p_ref,
                  wg_ref, bg_ref, wa_ref, wb_ref, wo_ref, o_ref):
    n_sub = x_ref.shape[1] // SUB_ROWS
    mod = mod_ref[0]
    g_in = gpre_ref[...] * (1.0 + mod[:, D_MODEL:2 * D_MODEL])
    g_out = gpost_ref[...] * mod[:, 2 * D_MODEL:3 * D_MODEL]
    n_chunks = D_MODEL // GATE_CHUNK
    lead = min(3, n_chunks)
    state = [dict() for _ in range(n_sub)]

    def gate_pair(st, c):
        lo = c * GATE_CHUNK
        g_a = _sigmoid(_dot(st["h"], wg_ref[:, lo:lo + GATE_CHUNK]) + bg_ref[:, lo:lo + GATE_CHUNK])
        lo += D_MODEL
        g_b = _sigmoid(_dot(st["h"], wg_ref[:, lo:lo + GATE_CHUNK]) + bg_ref[:, lo:lo + GATE_CHUNK])
        st["gates"].append((g_a, g_b))

    def out_partial(st, c):
        g_a, g_b = st["gates"][c]
        sl = slice(c * GATE_CHUNK, (c + 1) * GATE_CHUNK)
        part = _dot(_bf16(g_a * st["ya"][:, sl] + g_b * st["yb"][:, sl]), wo_ref[sl, :])
        st["y"] = part if c == 0 else st["y"] + part

    def stage(k, s):
        st = state[k]
        rows = slice(k * SUB_ROWS, (k + 1) * SUB_ROWS)
        if s == 0:
            st["ya"] = _dot(attn_ref[0, rows, :], wa_ref[...])
            st["yb"] = _dot(p_ref[0, rows, :], wb_ref[...])
        elif s <= lead:
            if s == 1:
                st["h"] = _bf16(_rms(x_ref[0, rows, :], g_in) + mod[:, :D_MODEL])
                st["gates"] = []
            gate_pair(st, s - 1)
        else:
            c = s - lead - 1
            if c + lead < n_chunks:
                gate_pair(st, c + lead)
            out_partial(st, c)
            if c == n_chunks - 1:
                o_ref[0, rows, :] = x_ref[0, rows, :] + _rms(st["y"], g_out)
                st.clear()

    n_stages = lead + 1 + n_chunks
    for step in range(n_stages + MERGE_LAG * (n_sub - 1)):
        for k in range(n_sub):
            s = step - k * MERGE_LAG
            if 0 <= s < n_stages:
                stage(k, s)


def _merge_call(x, mod3, g_pre, g_post, attn, pooled, w_in, b_gate, w_a, w_b, w_o, tile):
    b, s, d = x.shape
    assert w_in.shape[1] == 2 * GATE_WIDTH

    def const(shape):
        return pl.BlockSpec(shape, lambda i, t: (0,) * len(shape))

    return pl.pallas_call(
        _merge_kernel,
        grid=(b, s // tile),
        in_specs=[pl.BlockSpec((1, tile, d), lambda i, t: (i, t, 0)),
                  pl.BlockSpec((1, 1, N_MOD * d), lambda i, t: (i, 0, 0)),
                  const((1, d)), const((1, d)),
                  pl.BlockSpec((1, tile, ATTN_WIDTH), lambda i, t: (i, t, 0)),
                  pl.BlockSpec((1, tile, POOL_WIDTH), lambda i, t: (i, t, 0)),
                  pl.BlockSpec((d, GATE_WIDTH), lambda i, t: (0, 1)), const((1, GATE_WIDTH)),
                  const((ATTN_WIDTH, d)), const((POOL_WIDTH, d)), const((d, d))],
        out_specs=pl.BlockSpec((1, tile, d), lambda i, t: (i, t, 0)),
        out_shape=jax.ShapeDtypeStruct((b, s, d), jnp.float32),
        compiler_params=pltpu.CompilerParams(vmem_limit_bytes=VMEM_LIMIT),
        name="merge",
    )(x, mod3, g_pre, g_post, attn, pooled, w_in, b_gate, w_a, w_b, w_o)


def _ffn_kernel(x_ref, mod_ref, gpre_ref, gpost_ref, wu_ref, wd_ref, o_ref, *, d_ff, chunk):
    mod = mod_ref[0]
    g_in = gpre_ref[...] * (1.0 + mod[:, 4 * D_MODEL:5 * D_MODEL])
    g_out = gpost_ref[...] * mod[:, 5 * D_MODEL:]
    subs = list(range(0, x_ref.shape[1], SUB_ROWS))
    n_chunks = d_ff // chunk
    lag = n_chunks // 2
    hs, accs = {}, {}

    def ffn_chunk(r, c):
        if c == 0:
            hs[r] = _bf16(_rms(x_ref[0, r:r + SUB_ROWS, :], g_in) + mod[:, 3 * D_MODEL:4 * D_MODEL])
        c0 = c * chunk
        gate = _dot(hs[r], wu_ref[:, c0:c0 + chunk])
        up = _dot(hs[r], wu_ref[:, d_ff + c0:d_ff + c0 + chunk])
        part = _dot(_bf16(gate * _sigmoid(gate) * up), wd_ref[c0:c0 + chunk, :])
        accs[r] = part if c == 0 else accs[r] + part
        if c == n_chunks - 1:
            o_ref[0, r:r + SUB_ROWS, :] = x_ref[0, r:r + SUB_ROWS, :] + _rms(accs[r], g_out)

    for step in range(n_chunks + lag * (len(subs) - 1)):
        for idx, r in enumerate(subs):
            c = step - idx * lag
            if 0 <= c < n_chunks:
                ffn_chunk(r, c)


def _ffn_call(x, mod3, g_pre, g_post, w_up, w_down, tile):
    b, s, d = x.shape
    d_ff = w_down.shape[0]
    chunk = 2 * LANES
    assert d_ff % chunk == 0

    def const(shape):
        return pl.BlockSpec(shape, lambda i, t: (0,) * len(shape))

    return pl.pallas_call(
        functools.partial(_ffn_kernel, d_ff=d_ff, chunk=chunk),
        grid=(b, s // tile),
        in_specs=[pl.BlockSpec((1, tile, d), lambda i, t: (i, t, 0)),
                  pl.BlockSpec((1, 1, N_MOD * d), lambda i, t: (i, 0, 0)),
                  const((1, d)), const((1, d)),
                  pl.BlockSpec((d, 2 * d_ff), lambda i, t: (0, 0), pipeline_mode=pl.Buffered(1)),
                  pl.BlockSpec((d_ff, d), lambda i, t: (0, 0), pipeline_mode=pl.Buffered(1))],
        out_specs=pl.BlockSpec((1, tile, d), lambda i, t: (i, t, 0)),
        out_shape=jax.ShapeDtypeStruct((b, s, d), jnp.float32),
        compiler_params=pltpu.CompilerParams(vmem_limit_bytes=VMEM_LIMIT),
        name="ffn",
    )(x, mod3, g_pre, g_post, w_up, w_down)


def _rope_tables(n_tok):
    t = np.arange(n_tok)
    pos = np.stack([t // GRID_W, t % GRID_W], axis=-1).astype(np.float64)
    inv = ROPE_THETA ** (-np.arange(ROPE_FREQS, dtype=np.float64) / ROPE_FREQS)
    ang = pos[:, :, None] * inv
    cos, sin = np.cos(ang), np.sin(ang)
    cos_h = np.concatenate([cos[:, 0], cos[:, 0], cos[:, 1], cos[:, 1]], axis=-1)
    sin_h = np.concatenate([-sin[:, 0], sin[:, 0], -sin[:, 1], sin[:, 1]], axis=-1)
    cos_p, sin_p = np.tile(cos_h, (1, HEAD_PAIR)), np.tile(sin_h, (1, HEAD_PAIR))
    q_scale = HEAD_DIM ** -0.5 * LOG2E
    return jnp.asarray(np.stack([cos_p * q_scale, sin_p * q_scale, cos_p, sin_p]).astype(np.float32))


def kernel(x, c, ctx, c_ctx, w_ada, b_ada, g_pre_mix, g_post_mix, g_pre_ffn, g_post_ffn, w_in, b_gate, rpb,
           pool_w, pool_scale, w_proj_a, w_proj_b, w_out, w_up, w_down):
    b, s, d = x.shape
    depth = w_ada.shape[0]
    assert depth == 1, "context stream updates are only needed for depth > 1"
    rows = s // GRID_W
    a = ATTN_WIDTH

    c8 = jnp.zeros((8, d), jnp.float32).at[:b].set(c).at[b].set(c_ctx)
    mod = _mod_call(c8, w_ada[0], b_ada[0])
    mod3 = mod[:b].reshape(b, 1, N_MOD * d)
    mod_ctx = mod[b:b + 1, :2 * d].reshape(1, 1, 2 * d)

    w_in_b = _bf16(w_in[0])
    g_pre = g_pre_mix[0].reshape(1, d)
    k_c, v_c = _ctx_kv_call(ctx, mod_ctx, g_pre, w_in_b)
    q, k, v, pooled = _proj_call(x, mod3, g_pre, w_in_b, _rope_tables(s), tile=PROJ_TILE)
    bias = _attn_bias_table(rpb[0], rows)
    attn = _attn_call(q, k, v, k_c, v_c, bias).reshape(b, s, a)
    w_pool = _pool_fold_call(pool_w[0], pool_scale[0], w_proj_b[0])
    x1 = _merge_call(x, mod3, g_pre, g_post_mix[0].reshape(1, d), attn, pooled, w_in_b, b_gate[0].reshape(1, -1),
                     _bf16(w_proj_a[0]), w_pool, _bf16(w_out[0]), tile=2 * SUB_ROWS)
    return _ffn_call(x1, mod3, g_pre_ffn[0].reshape(1, d), g_post_ffn[0].reshape(1, d),
                     _bf16(w_up[0]), _bf16(w_down[0]), tile=2 * SUB_ROWS)
```

```python
import functools

import jax
import jax.numpy as jnp
import numpy as np
from jax.experimental import pallas as pl
from jax.experimental.pallas import tpu as pltpu

D_MODEL = 1024
GRID_W = 64
N_HEADS = 8
HEAD_DIM = 64
ATTN_WIDTH = N_HEADS * HEAD_DIM
POOL_WINDOWS = (2, 4, 8, 16)
N_POOL_GROUPS = len(POOL_WINDOWS)
POOL_WIDTH = 512
POOL_GROUP = POOL_WIDTH // N_POOL_GROUPS
GATE_WIDTH = 2 * D_MODEL
WIN_ROWS = 8
WIN_COLS = 16
Q_COL_BLOCK = 16
K_COL_BLOCK = Q_COL_BLOCK + WIN_COLS
ROPE_FREQS = HEAD_DIM // 4
ROPE_THETA = 10000.0
N_MOD = 6
EPS = 1e-6
NEG_INF = -1e30

LANES = 128
BF16_SUBLANES = 16
LOG2E = 1.4426950408889634
HEAD_PAIR = LANES // HEAD_DIM
N_HEAD_PAIRS = N_HEADS // HEAD_PAIR
POOL_HALO = max(POOL_WINDOWS) // 2
Q_ROWS = 8
BAND_ROWS = 16
KEY_CHUNK = 256
VAL_CHUNK = 256
GATE_CHUNK = 512
PROJ_TILE = 1024
PROJ_SUB_ROWS = 1024
SUB_ROWS = 512
MERGE_LAG = 2
VALUE_LAG = 2
N_COL_BLOCKS = GRID_W // Q_COL_BLOCK
COL_CLASS = tuple(0 if n == 0 else 2 if n == N_COL_BLOCKS - 1 else 1 for n in range(N_COL_BLOCKS))
N_COL_CLASSES = 3
VMEM_LIMIT = 56 * 1024 * 1024


def _bf16(a):
    return a.astype(jnp.bfloat16)


def _dot(a, b):
    return jnp.dot(a, b, preferred_element_type=jnp.float32)


def _rms(xf, g):
    return (xf * jax.lax.rsqrt(jnp.mean(xf * xf, axis=-1, keepdims=True) + EPS)) * g


def _sigmoid(z):
    return 1.0 / (1.0 + jnp.exp2(z * -LOG2E))


def _mod_kernel(c_ref, w_ref, b_ref, o_ref):
    c = c_ref[...]
    s = c * _sigmoid(c)
    o_ref[...] = _dot(_bf16(s), _bf16(w_ref[...])) + b_ref[...]


def _mod_call(c8, w_ada, b_ada):
    n = w_ada.shape[1]
    bn = 2 * D_MODEL
    return pl.pallas_call(
        _mod_kernel,
        grid=(n // bn,),
        in_specs=[pl.BlockSpec((8, D_MODEL), lambda j: (0, 0)),
                  pl.BlockSpec((D_MODEL, bn), lambda j: (0, j)),
                  pl.BlockSpec((1, bn), lambda j: (0, j))],
        out_specs=pl.BlockSpec((8, bn), lambda j: (0, j)),
        out_shape=jax.ShapeDtypeStruct((8, n), jnp.float32),
        name="mod",
    )(c8, w_ada, b_ada.reshape(1, n))


def _ctx_kv_kernel(ctx_ref, mod_ref, g_ref, wk_ref, wv_ref, k_ref, v_ref):
    mod = mod_ref[0]
    gain = g_ref[...] * (1.0 + mod[:, D_MODEL:2 * D_MODEL])
    b, l, d = ctx_ref.shape
    h = _bf16(_rms(ctx_ref[...].reshape(b * l, d), gain) + mod[:, :D_MODEL])
    k_ref[...] = _bf16(_dot(h, wk_ref[...])).reshape(b, l, ATTN_WIDTH)
    v_ref[...] = _bf16(_dot(h, wv_ref[...])).reshape(b, l, ATTN_WIDTH)


def _ctx_kv_call(ctx, mod_ctx, g_pre, w_in):
    b, l, d = ctx.shape
    out = jax.ShapeDtypeStruct((b, l, ATTN_WIDTH), jnp.bfloat16)
    return pl.pallas_call(
        _ctx_kv_kernel,
        grid=(1,),
        in_specs=[pl.BlockSpec((b, l, d), lambda i: (0, 0, 0)),
                  pl.BlockSpec((1, 1, 2 * d), lambda i: (0, 0, 0)),
                  pl.BlockSpec((1, d), lambda i: (0, 0)),
                  pl.BlockSpec((d, ATTN_WIDTH), lambda i: (0, 1)),
                  pl.BlockSpec((d, ATTN_WIDTH), lambda i: (0, 2))],
        out_specs=[pl.BlockSpec((b, l, ATTN_WIDTH), lambda i: (0, 0, 0))] * 2,
        out_shape=[out, out],
        name="ctx_kv",
    )(ctx, mod_ctx, g_pre, w_in, w_in)


def _rope(a, cos, sin_signed, first_half):
    swapped = jnp.where(first_half, pltpu.roll(a, LANES - ROPE_FREQS, 1), pltpu.roll(a, ROPE_FREQS, 1))
    return a * cos + swapped * sin_signed


def _window_means_minus_token(ext, tok0, seq):
    halo = POOL_HALO
    n_ext = ext.shape[0]
    n_tok = n_ext - 2 * halo
    tok_head = tok0 + jax.lax.broadcasted_iota(jnp.int32, (halo, POOL_GROUP), 0)
    tok_tail = tok_head + (n_tok - halo)
    pooled = []
    for g, w in enumerate(POOL_WINDOWS):
        e = ext[:, g * POOL_GROUP:(g + 1) * POOL_GROUP]
        acc = e
        span = 1
        while 2 * span < w:
            acc = acc + pltpu.roll(acc, n_ext - span, 0)
            span *= 2
        acc = acc + pltpu.roll(acc, span, 0)

        def window_count(tok):
            return (jnp.minimum(tok + (w - w // 2), seq) - jnp.maximum(tok - w // 2, 0)).astype(jnp.float32)

        mean = jnp.concatenate([acc[halo:2 * halo] / window_count(tok_head),
                                acc[2 * halo:n_tok] * (1.0 / w),
                                acc[n_tok:halo + n_tok] / window_count(tok_tail)], axis=0)
        pooled.append(mean - e[halo:halo + n_tok])
    return jnp.concatenate(pooled, axis=-1)


def _proj_kernel(x_ref, xprev_ref, xnext_ref, mod_ref, g_ref, w_ref, rope_ref,
                 q_ref, k_ref, v_ref, p_ref, *, seq):
    t = pl.program_id(1)
    tile = x_ref.shape[1]
    n_sub = tile // PROJ_SUB_ROWS
    halo = POOL_HALO
    mod = mod_ref[0]
    gain = g_ref[...] * (1.0 + mod[:, D_MODEL:2 * D_MODEL])
    shift = mod[:, :D_MODEL]
    lane = jax.lax.broadcasted_iota(jnp.int32, (PROJ_SUB_ROWS, LANES), 1)
    first_half = (lane % (2 * ROPE_FREQS)) < ROPE_FREQS
    hs = {}

    def pool_part(k):
        r = k * PROJ_SUB_ROWS
        hs[k] = _bf16(_rms(x_ref[0, r:r + PROJ_SUB_ROWS, :], gain) + shift)
        x_prev = xprev_ref[0, 0] if k == 0 else x_ref[0, r - halo:r, :]
        x_next = xnext_ref[0, 0] if k == n_sub - 1 else x_ref[0, r + PROJ_SUB_ROWS:r + PROJ_SUB_ROWS + halo, :]
        h_halo = _bf16(_rms(jnp.concatenate([x_prev, x_next], axis=0), gain) + shift)
        p_all = _dot(jnp.concatenate([hs[k], h_halo], axis=0), w_ref[:, 3 * ATTN_WIDTH:])
        p_prev = p_all[PROJ_SUB_ROWS:PROJ_SUB_ROWS + halo]
        p_next = p_all[PROJ_SUB_ROWS + halo:]
        if k == 0:
            p_prev = jnp.where(t == 0, 0.0, p_prev)
        if k == n_sub - 1:
            p_next = jnp.where(t == pl.num_programs(1) - 1, 0.0, p_next)
        ext = jnp.concatenate([p_prev, p_all[:PROJ_SUB_ROWS], p_next], axis=0)
        p_ref[0, r:r + PROJ_SUB_ROWS, :] = _bf16(_window_means_minus_token(ext, t * tile + r, seq))

    def qkv_part(k):
        r = k * PROJ_SUB_ROWS
        rows = slice(r, r + PROJ_SUB_ROWS)
        proj = _dot(hs.pop(k), w_ref[:, :3 * ATTN_WIDTH])
        cos_q, sin_q, cos_k, sin_k = (rope_ref[i, rows, :] for i in range(4))
        for j in range(N_HEAD_PAIRS):
            sl = slice(j * LANES, (j + 1) * LANES)
            q_ref[0, rows, sl] = _bf16(_rope(proj[:, sl], cos_q, sin_q, first_half))
            ksl = slice(ATTN_WIDTH + j * LANES, ATTN_WIDTH + (j + 1) * LANES)
            k_ref[0, rows, sl] = _bf16(_rope(proj[:, ksl], cos_k, sin_k, first_half))
        v_ref[0, rows, :] = _bf16(proj[:, 2 * ATTN_WIDTH:3 * ATTN_WIDTH])

    for k in range(n_sub):
        pool_part(k)
        qkv_part(k)


def _proj_call(x, mod3, g_pre, w_in, rope, tile):
    b, s, d = x.shape
    n = 3 * ATTN_WIDTH + POOL_WIDTH
    halo = POOL_HALO
    assert POOL_WIDTH == ATTN_WIDTH and 2 * halo == BF16_SUBLANES and tile % BF16_SUBLANES == 0
    x4 = x.reshape(b, s // halo, halo, d)
    n_halo = s // halo
    per = tile // halo
    act = jax.ShapeDtypeStruct((b, s, ATTN_WIDTH), jnp.bfloat16)
    act_spec = pl.BlockSpec((1, tile, ATTN_WIDTH), lambda i, t: (i, t, 0))
    return pl.pallas_call(
        functools.partial(_proj_kernel, seq=s),
        grid=(b, s // tile),
        in_specs=[pl.BlockSpec((1, tile, d), lambda i, t: (i, t, 0)),
                  pl.BlockSpec((1, 1, halo, d), lambda i, t: (i, jnp.maximum(t * per - 1, 0), 0, 0)),
                  pl.BlockSpec((1, 1, halo, d), lambda i, t: (i, jnp.minimum((t + 1) * per, n_halo - 1), 0, 0)),
                  pl.BlockSpec((1, 1, N_MOD * d), lambda i, t: (i, 0, 0)),
                  pl.BlockSpec((1, d), lambda i, t: (0, 0)),
                  pl.BlockSpec((d, n), lambda i, t: (0, 0)),
                  pl.BlockSpec((4, tile, LANES), lambda i, t: (0, t, 0))],
        out_specs=[act_spec] * 4,
        out_shape=[act] * 4,
        compiler_params=pltpu.CompilerParams(vmem_limit_bytes=VMEM_LIMIT),
        name="proj",
    )(x, x4, x4, mod3, g_pre, w_in, rope)


def _band_start(i, rows):
    return jnp.clip(i * Q_ROWS - WIN_ROWS // 2, 0, rows - BAND_ROWS)


def _key_col0(n):
    return min(max(n * Q_COL_BLOCK - WIN_COLS // 2, 0), GRID_W - K_COL_BLOCK)


def _attn_kernel(q_ref, k_ref, v_ref, kc_ref, vc_ref, bias_ref, o_ref, *, rows):
    i = pl.program_id(1)
    start = pl.multiple_of(_band_start(i, rows) * GRID_W, GRID_W)
    n_q = Q_ROWS * Q_COL_BLOCK
    n_k = BAND_ROWS * K_COL_BLOCK
    n_l = HEAD_PAIR * n_q
    n_ctx = kc_ref.shape[1]
    first_d = jax.lax.broadcasted_iota(jnp.int32, (LANES, n_q), 0) < HEAD_DIM
    zero = jnp.zeros((), jnp.bfloat16)
    ones_rows = jnp.ones((BF16_SUBLANES, VAL_CHUNK), jnp.bfloat16)
    per_val = VAL_CHUNK // KEY_CHUNK
    n_key_chunks = (n_k + n_ctx) // KEY_CHUNK
    units = [(hp, n) for hp in range(N_HEAD_PAIRS) for n in range(N_COL_BLOCKS)]
    state = [dict() for _ in units]

    def band_tile(ref, hp, n):
        c0 = _key_col0(n)
        band = ref[0, pl.ds(start, BAND_ROWS * GRID_W), hp * LANES:(hp + 1) * LANES]
        return band.reshape(BAND_ROWS, GRID_W, LANES)[:, c0:c0 + K_COL_BLOCK, :].reshape(n_k, LANES)

    def score_chunk(u, j):
        hp, n = units[u]
        st = state[u]
        if j == 0:
            q_t = q_ref[0, :, hp * LANES:(hp + 1) * LANES].reshape(Q_ROWS, GRID_W, LANES)
            q_t = q_t[:, n * Q_COL_BLOCK:(n + 1) * Q_COL_BLOCK, :].reshape(n_q, LANES).T
            st["q"] = jnp.concatenate([jnp.where(first_d, q_t, zero), jnp.where(first_d, zero, q_t)], axis=1)
            st["k"] = band_tile(k_ref, hp, n)
            st["s"] = []
        lo = j * KEY_CHUNK
        if lo < n_k:
            s_j = _dot(st["k"][lo:lo + KEY_CHUNK], st["q"]) + bias_ref[0, hp, COL_CLASS[n], lo:lo + KEY_CHUNK, :]
        else:
            s_j = _dot(kc_ref[0, lo - n_k:lo - n_k + KEY_CHUNK, hp * LANES:(hp + 1) * LANES], st["q"])
        st["s"].append(s_j)
        m_j = jnp.max(s_j.reshape(KEY_CHUNK // 8, 8, n_l), axis=0)
        st["m"] = m_j if j == 0 else jnp.maximum(st["m"], m_j)

    def value_chunk(u, jv):
        hp, n = units[u]
        st = state[u]
        if jv == 0:
            st["m"] = jnp.max(st["m"], axis=0, keepdims=True)
            st["v"] = band_tile(v_ref, hp, n)
        p_v = [_bf16(jnp.exp2(st["s"][j] - st["m"])) for j in range(jv * per_val, (jv + 1) * per_val)]
        lo = jv * VAL_CHUNK
        if lo < n_k:
            v_j = st["v"][lo:lo + VAL_CHUNK]
        else:
            v_j = vc_ref[0, lo - n_k:lo - n_k + VAL_CHUNK, hp * LANES:(hp + 1) * LANES]
        v_ext = jnp.concatenate([v_j.T, ones_rows], axis=0)
        o_j = _dot(v_ext, jnp.concatenate(p_v, axis=0))
        st["o"] = o_j if jv == 0 else st["o"] + o_j

    def finish(u):
        hp, n = units[u]
        st = state[u]
        o_t = st["o"]
        inv_l = 1.0 / o_t[LANES:LANES + 1, :]
        o_t = jnp.concatenate([o_t[:HEAD_DIM, :n_q] * inv_l[:, :n_q],
                               o_t[HEAD_DIM:LANES, n_q:] * inv_l[:, n_q:]], axis=0)
        o_ref[0, :, n * Q_COL_BLOCK:(n + 1) * Q_COL_BLOCK, hp * LANES:(hp + 1) * LANES] = (
            _bf16(o_t.T).reshape(Q_ROWS, Q_COL_BLOCK, LANES))
        st.clear()

    n_val_chunks = n_key_chunks // per_val
    n_slots = len(units) * n_val_chunks
    for slot in range(n_slots + n_val_chunks + VALUE_LAG):
        if slot < n_slots:
            u, jv = divmod(slot, n_val_chunks)
            for j in range(jv * per_val, (jv + 1) * per_val):
                score_chunk(u, j)
        done = slot - n_val_chunks - VALUE_LAG
        if done >= 0:
            u, jv = divmod(done, n_val_chunks)
            value_chunk(u, jv)
            if jv == n_val_chunks - 1:
                finish(u)


def _attn_call(q, k, v, k_c, v_c, bias):
    b, s, _ = q.shape
    rows = s // GRID_W
    n_i = rows // Q_ROWS
    l = k_c.shape[1]
    tok = Q_ROWS * GRID_W

    def row_class(i):
        return jnp.where(i == 0, 0, jnp.where(i == n_i - 1, 2, 1))

    assert (BAND_ROWS * K_COL_BLOCK) % VAL_CHUNK == 0 and l % VAL_CHUNK == 0 and VAL_CHUNK % KEY_CHUNK == 0
    whole_seq = pl.Buffered(2)
    return pl.pallas_call(
        functools.partial(_attn_kernel, rows=rows),
        grid=(b, n_i),
        in_specs=[pl.BlockSpec((1, tok, ATTN_WIDTH), lambda bi, i: (bi, i, 0)),
                  pl.BlockSpec((1, s, ATTN_WIDTH), lambda bi, i: (bi, 0, 0), pipeline_mode=whole_seq),
                  pl.BlockSpec((1, s, ATTN_WIDTH), lambda bi, i: (bi, 0, 0), pipeline_mode=whole_seq),
                  pl.BlockSpec((1, l, ATTN_WIDTH), lambda bi, i: (bi, 0, 0)),
                  pl.BlockSpec((1, l, ATTN_WIDTH), lambda bi, i: (bi, 0, 0)),
                  pl.BlockSpec((1, N_HEAD_PAIRS, N_COL_CLASSES, BAND_ROWS * K_COL_BLOCK,
                                HEAD_PAIR * Q_ROWS * Q_COL_BLOCK),
                               lambda bi, i: (row_class(i), 0, 0, 0, 0))],
        out_specs=pl.BlockSpec((1, Q_ROWS, GRID_W, ATTN_WIDTH), lambda bi, i: (bi, i, 0, 0)),
        out_shape=jax.ShapeDtypeStruct((b, rows, GRID_W, ATTN_WIDTH), jnp.bfloat16),
        compiler_params=pltpu.CompilerParams(vmem_limit_bytes=VMEM_LIMIT),
        name="attn",
    )(q, k, v, k_c, v_c, bias)


N_RPB_ROWS = 2 * WIN_ROWS - 1
N_RPB_COLS = 2 * WIN_COLS - 1
ROW_CLASS_STEPS = (0, 1, -1)


def _row_class_geometry(i, rows):
    start = min(max(i * Q_ROWS - WIN_ROWS // 2, 0), rows - BAND_ROWS)
    r0 = [min(max(i * Q_ROWS + a - WIN_ROWS // 2, 0), rows - WIN_ROWS) - start for a in range(Q_ROWS)]
    return start, start - i * Q_ROWS + WIN_ROWS - 1, r0


def _bias_kernel(rpb_ref, o_ref, toep_ref, *, rows):
    hp = pl.program_id(0)
    col_class = pl.program_id(1)
    n = jnp.where(col_class == 0, 0, jnp.where(col_class == N_COL_CLASSES - 1, N_COL_BLOCKS - 1, 1))
    c0 = jnp.clip(n * Q_COL_BLOCK - WIN_COLS // 2, 0, GRID_W - K_COL_BLOCK)
    kc = c0 + jax.lax.broadcasted_iota(jnp.int32, (K_COL_BLOCK, LANES), 0)
    lane = jax.lax.broadcasted_iota(jnp.int32, (K_COL_BLOCK, LANES), 1)
    q_row = lane // Q_COL_BLOCK
    qc = n * Q_COL_BLOCK + lane % Q_COL_BLOCK
    q_col0 = jnp.clip(qc - WIN_COLS // 2, 0, GRID_W - WIN_COLS)
    col_valid = (kc >= q_col0) & (kc < q_col0 + WIN_COLS)
    cidx = jnp.clip(kc - qc + WIN_COLS - 1, 0, N_RPB_COLS - 1)
    masked = jnp.full((K_COL_BLOCK, LANES), NEG_INF, jnp.float32)
    n_i = rows // Q_ROWS
    for e in range(HEAD_PAIR):
        base = (hp * HEAD_PAIR + e) * N_RPB_ROWS
        for ri in range(N_RPB_ROWS):
            rpb_row = jnp.broadcast_to(rpb_ref[pl.ds(base + ri, 1), :], (K_COL_BLOCK, LANES))
            picked = jnp.take_along_axis(rpb_row, cidx, axis=1)
            toep_ref[ri] = jnp.where(col_valid, picked * LOG2E, masked)
        for rc, step in enumerate(ROW_CLASS_STEPS):
            _, ridx0, r0 = _row_class_geometry(step % n_i, rows)
            for ar in range(BAND_ROWS):
                slab = masked
                for a in range(Q_ROWS):
                    if 0 <= ar - r0[a] < WIN_ROWS:
                        slab = jnp.where(q_row == a, toep_ref[ar - a + ridx0], slab)
                o_ref[rc, 0, 0, ar * K_COL_BLOCK:(ar + 1) * K_COL_BLOCK, e * LANES:(e + 1) * LANES] = slab


def _attn_bias_table(rpb, rows):
    assert rows // Q_ROWS >= 3 and rows % Q_ROWS == 0
    assert Q_ROWS * Q_COL_BLOCK == LANES
    for n in range(1, N_COL_BLOCKS - 1):
        assert _key_col0(n) == n * Q_COL_BLOCK - WIN_COLS // 2
        assert n * Q_COL_BLOCK >= WIN_COLS // 2 and (n + 1) * Q_COL_BLOCK - 1 - WIN_COLS // 2 <= GRID_W - WIN_COLS
    n_q, n_k = HEAD_PAIR * Q_ROWS * Q_COL_BLOCK, BAND_ROWS * K_COL_BLOCK
    n_rc = len(ROW_CLASS_STEPS)
    return pl.pallas_call(
        functools.partial(_bias_kernel, rows=rows),
        grid=(N_HEAD_PAIRS, N_COL_CLASSES),
        in_specs=[pl.BlockSpec((N_HEADS * N_RPB_ROWS, LANES), lambda hp, n: (0, 0))],
        out_specs=pl.BlockSpec((n_rc, 1, 1, n_k, n_q), lambda hp, n: (0, hp, n, 0, 0)),
        out_shape=jax.ShapeDtypeStruct((n_rc, N_HEAD_PAIRS, N_COL_CLASSES, n_k, n_q), jnp.float32),
        scratch_shapes=[pltpu.VMEM((N_RPB_ROWS, K_COL_BLOCK, LANES), jnp.float32)],
        name="bias",
    )(jnp.pad(rpb.reshape(N_HEADS * N_RPB_ROWS, N_RPB_COLS), ((0, 0), (0, LANES - N_RPB_COLS))))


def _pool_fold_kernel(pw_ref, ps_ref, wb_ref, o_ref):
    o_ref[...] = _bf16(jnp.dot(pw_ref[0] * ps_ref[0], wb_ref[...], precision=jax.lax.Precision.HIGHEST,
                               preferred_element_type=jnp.float32))


def _pool_fold_call(pool_w, pool_scale, w_proj_b):
    d = w_proj_b.shape[1]
    return pl.pallas_call(
        _pool_fold_kernel,
        grid=(N_POOL_GROUPS,),
        in_specs=[pl.BlockSpec((1, POOL_GROUP, POOL_GROUP), lambda g: (g, 0, 0)),
                  pl.BlockSpec((1, 1, POOL_GROUP), lambda g: (g, 0, 0)),
                  pl.BlockSpec((POOL_GROUP, d), lambda g: (g, 0))],
        out_specs=pl.BlockSpec((POOL_GROUP, d), lambda g: (g, 0)),
        out_shape=jax.ShapeDtypeStruct((POOL_WIDTH, d), jnp.bfloat16),
        name="pool_fold",
    )(pool_w, pool_scale.reshape(N_POOL_GROUPS, 1, POOL_GROUP), w_proj_b)


def _merge_kernel(x_ref, mod_ref, gpre_ref, gpost_ref, attn_ref, p_ref,
                  wg_ref, bg_ref, wa_ref, wb_ref, wo_ref, o_ref):
    n_sub = x_ref.shape[1] // SUB_ROWS
    mod = mod_ref[0]
    g_in = gpre_ref[...] * (1.0 + mod[:, D_MODEL:2 * D_MODEL])
    g_out = gpost_ref[...] * mod[:, 2 * D_MODEL:3 * D_MODEL]
    n_chunks = D_MODEL // GATE_CHUNK
    lead = min(3, n_chunks)
    state = [dict() for _ in range(n_sub)]

    def gate_pair(st, c):
        lo = c * GATE_CHUNK
        g_a = _sigmoid(_dot(st["h"], wg_ref[:, lo:lo + GATE_CHUNK]) + bg_ref[:, lo:lo + GATE_CHUNK])
        lo += D_MODEL
        g_b = _sigmoid(_dot(st["h"], wg_ref[:, lo:lo + GATE_CHUNK]) + bg_ref[:, lo:lo + GATE_CHUNK])
        st["gates"].append((g_a, g_b))

    def out_partial(st, c):
        g_a, g_b = st["gates"][c]
        sl = slice(c * GATE_CHUNK, (c + 1) * GATE_CHUNK)
        part = _dot(_bf16(g_a * st["ya"][:, sl] + g_b * st["yb"][:, sl]), wo_ref[sl, :])
        st["y"] = part if c == 0 else st["y"] + part

    def stage(k, s):
        st = state[k]
        rows = slice(k * SUB_ROWS, (k + 1) * SUB_ROWS)
        if s == 0:
            st["ya"] = _dot(attn_ref[0, rows, :], wa_ref[...])
            st["yb"] = _dot(p_ref[0, rows, :], wb_ref[...])
        elif s <= lead:
            if s == 1:
                st["h"] = _bf16(_rms(x_ref[0, rows, :], g_in) + mod[:, :D_MODEL])
                st["gates"] = []
            gate_pair(st, s - 1)
        else:
            c = s - lead - 1
            if c + lead < n_chunks:
                gate_pair(st, c + lead)
            out_partial(st, c)
            if c == n_chunks - 1:
                o_ref[0, rows, :] = x_ref[0, rows, :] + _rms(st["y"], g_out)
                st.clear()

    n_stages = lead + 1 + n_chunks
    for step in range(n_stages + MERGE_LAG * (n_sub - 1)):
        for k in range(n_sub):
            s = step - k * MERGE_LAG
            if 0 <= s < n_stages:
                stage(k, s)


def _merge_call(x, mod3, g_pre, g_post, attn, pooled, w_in, b_gate, w_a, w_b, w_o, tile):
    b, s, d = x.shape
    assert w_in.shape[1] == 2 * GATE_WIDTH

    def const(shape):
        return pl.BlockSpec(shape, lambda i, t: (0,) * len(shape))

    return pl.pallas_call(
        _merge_kernel,
        grid=(b, s // tile),
        in_specs=[pl.BlockSpec((1, tile, d), lambda i, t: (i, t, 0)),
                  pl.BlockSpec((1, 1, N_MOD * d), lambda i, t: (i, 0, 0)),
                  const((1, d)), const((1, d)),
                  pl.BlockSpec((1, tile, ATTN_WIDTH), lambda i, t: (i, t, 0)),
                  pl.BlockSpec((1, tile, POOL_WIDTH), lambda i, t: (i, t, 0)),
                  pl.BlockSpec((d, GATE_WIDTH), lambda i, t: (0, 1)), const((1, GATE_WIDTH)),
                  const((ATTN_WIDTH, d)), const((POOL_WIDTH, d)), const((d, d))],
        out_specs=pl.BlockSpec((1, tile, d), lambda i, t: (i, t, 0)),
        out_shape=jax.ShapeDtypeStruct((b, s, d), jnp.float32),
        compiler_params=pltpu.CompilerParams(vmem_limit_bytes=VMEM_LIMIT),
        name="merge",
    )(x, mod3, g_pre, g_post, attn, pooled, w_in, b_gate, w_a, w_b, w_o)


def _ffn_kernel(x_ref, mod_ref, gpre_ref, gpost_ref, wu_ref, wd_ref, o_ref, *, d_ff, chunk):
    mod = mod_ref[0]
    g_in = gpre_ref[...] * (1.0 + mod[:, 4 * D_MODEL:5 * D_MODEL])
    g_out = gpost_ref[...] * mod[:, 5 * D_MODEL:]
    subs = list(range(0, x_ref.shape[1], SUB_ROWS))
    n_chunks = d_ff // chunk
    lag = n_chunks // 2
    hs, accs = {}, {}

    def ffn_chunk(r, c):
        if c == 0:
            hs[r] = _bf16(_rms(x_ref[0, r:r + SUB_ROWS, :], g_in) + mod[:, 3 * D_MODEL:4 * D_MODEL])
        c0 = c * chunk
        gate = _dot(hs[r], wu_ref[:, c0:c0 + chunk])
        up = _dot(hs[r], wu_ref[:, d_ff + c0:d_ff + c0 + chunk])
        part = _dot(_bf16(gate * _sigmoid(gate) * up), wd_ref[c0:c0 + chunk, :])
        accs[r] = part if c == 0 else accs[r] + part
        if c == n_chunks - 1:
            o_ref[0, r:r + SUB_ROWS, :] = x_ref[0, r:r + SUB_ROWS, :] + _rms(accs[r], g_out)

    for step in range(n_chunks + lag * (len(subs) - 1)):
        for idx, r in enumerate(subs):
            c = step - idx * lag
            if 0 <= c < n_chunks:
                ffn_chunk(r, c)


def _ffn_call(x, mod3, g_pre, g_post, w_up, w_down, tile):
    b, s, d = x.shape
    d_ff = w_down.shape[0]
    chunk = 2 * LANES
    assert d_ff % chunk == 0

    def const(shape):
        return pl.BlockSpec(shape, lambda i, t: (0,) * len(shape))

    return pl.pallas_call(
        functools.partial(_ffn_kernel, d_ff=d_ff, chunk=chunk),
        grid=(b, s // tile),
        in_specs=[pl.BlockSpec((1, tile, d), lambda i, t: (i, t, 0)),
                  pl.BlockSpec((1, 1, N_MOD * d), lambda i, t: (i, 0, 0)),
                  const((1, d)), const((1, d)),
                  pl.BlockSpec((d, 2 * d_ff), lambda i, t: (0, 0), pipeline_mode=pl.Buffered(1)),
                  pl.BlockSpec((d_ff, d), lambda i, t: (0, 0), pipeline_mode=pl.Buffered(1))],
        out_specs=pl.BlockSpec((1, tile, d), lambda i, t: (i, t, 0)),
        out_shape=jax.ShapeDtypeStruct((b, s, d), jnp.float32),
        compiler_params=pltpu.CompilerParams(vmem_limit_bytes=VMEM_LIMIT),
        name="ffn",
    )(x, mod3, g_pre, g_post, w_up, w_down)


def _rope_tables(n_tok):
    t = np.arange(n_tok)
    pos = np.stack([t // GRID_W, t % GRID_W], axis=-1).astype(np.float64)
    inv = ROPE_THETA ** (-np.arange(ROPE_FREQS, dtype=np.float64) / ROPE_FREQS)
    ang = pos[:, :, None] * inv
    cos, sin = np.cos(ang), np.sin(ang)
    cos_h = np.concatenate([cos[:, 0], cos[:, 0], cos[:, 1], cos[:, 1]], axis=-1)
    sin_h = np.concatenate([-sin[:, 0], sin[:, 0], -sin[:, 1], sin[:, 1]], axis=-1)
    cos_p, sin_p = np.tile(cos_h, (1, HEAD_PAIR)), np.tile(sin_h, (1, HEAD_PAIR))
    q_scale = HEAD_DIM ** -0.5 * LOG2E
    return jnp.asarray(np.stack([cos_p * q_scale, sin_p * q_scale, cos_p, sin_p]).astype(np.float32))


def kernel(x, c, ctx, c_ctx, w_ada, b_ada, g_pre_mix, g_post_mix, g_pre_ffn, g_post_ffn, w_in, b_gate, rpb,
           pool_w, pool_scale, w_proj_a, w_proj_b, w_out, w_up, w_down):
    b, s, d = x.shape
    depth = w_ada.shape[0]
    assert depth == 1, "context stream updates are only needed for depth > 1"
    rows = s // GRID_W
    a = ATTN_WIDTH

    c8 = jnp.zeros((8, d), jnp.float32).at[:b].set(c).at[b].set(c_ctx)
    mod = _mod_call(c8, w_ada[0], b_ada[0])
    mod3 = mod[:b].reshape(b, 1, N_MOD * d)
    mod_ctx = mod[b:b + 1, :2 * d].reshape(1, 1, 2 * d)

    w_in_b = _bf16(w_in[0])
    g_pre = g_pre_mix[0].reshape(1, d)
    k_c, v_c = _ctx_kv_call(ctx, mod_ctx, g_pre, w_in_b)
    q, k, v, pooled = _proj_call(x, mod3, g_pre, w_in_b, _rope_tables(s), tile=PROJ_TILE)
    bias = _attn_bias_table(rpb[0], rows)
    attn = _attn_call(q, k, v, k_c, v_c, bias).reshape(b, s, a)
    w_pool = _pool_fold_call(pool_w[0], pool_scale[0], w_proj_b[0])
    x1 = _merge_call(x, mod3, g_pre, g_post_mix[0].reshape(1, d), attn, pooled, w_in_b, b_gate[0].reshape(1, -1),
                     _bf16(w_proj_a[0]), w_pool, _bf16(w_out[0]), tile=2 * SUB_ROWS)
    return _ffn_call(x1, mod3, g_pre_ffn[0].reshape(1, d), g_post_ffn[0].reshape(1, d),
                     _bf16(w_up[0]), _bf16(w_down[0]), tile=2 * SUB_ROWS)
```

```python
import functools

import jax
import jax.numpy as jnp
import numpy as np
from jax.experimental import pallas as pl
from jax.experimental.pallas import tpu as pltpu

D_MODEL = 1024
GRID_W = 64
N_HEADS = 8
HEAD_DIM = 64
ATTN_WIDTH = N_HEADS * HEAD_DIM
POOL_WINDOWS = (2, 4, 8, 16)
N_POOL_GROUPS = len(POOL_WINDOWS)
POOL_WIDTH = 512
POOL_GROUP = POOL_WIDTH // N_POOL_GROUPS
GATE_WIDTH = 2 * D_MODEL
WIN_ROWS = 8
WIN_COLS = 16
Q_COL_BLOCK = 16
K_COL_BLOCK = Q_COL_BLOCK + WIN_COLS
ROPE_FREQS = HEAD_DIM // 4
ROPE_THETA = 10000.0
N_MOD = 6
EPS = 1e-6
NEG_INF = -1e30

LANES = 128
BF16_SUBLANES = 16
LOG2E = 1.4426950408889634
HEAD_PAIR = LANES // HEAD_DIM
N_HEAD_PAIRS = N_HEADS // HEAD_PAIR
POOL_HALO = max(POOL_WINDOWS) // 2
Q_ROWS = 8
BAND_ROWS = 16
KEY_CHUNK = 256
VAL_CHUNK = 256
GATE_CHUNK = 512
PROJ_TILE = 1024
PROJ_SUB_ROWS = 1024
SUB_ROWS = 512
MERGE_LAG = 2
VALUE_LAG = 2
N_COL_BLOCKS = GRID_W // Q_COL_BLOCK
COL_CLASS = tuple(0 if n == 0 else 2 if n == N_COL_BLOCKS - 1 else 1 for n in range(N_COL_BLOCKS))
N_COL_CLASSES = 3
VMEM_LIMIT = 56 * 1024 * 1024


def _bf16(a):
    return a.astype(jnp.bfloat16)


def _dot(a, b):
    return jnp.dot(a, b, preferred_element_type=jnp.float32)


def _rms(xf, g):
    return (xf * jax.lax.rsqrt(jnp.mean(xf * xf, axis=-1, keepdims=True) + EPS)) * g


def _sigmoid(z):
    return 1.0 / (1.0 + jnp.exp2(z * -LOG2E))


def _mod_kernel(c_ref, w_ref, b_ref, o_ref):
    c = c_ref[...]
    s = c * _sigmoid(c)
    o_ref[...] = _dot(_bf16(s), _bf16(w_ref[...])) + b_ref[...]


def _mod_call(c8, w_ada, b_ada):
    n = w_ada.shape[1]
    bn = 2 * D_MODEL
    return pl.pallas_call(
        _mod_kernel,
        grid=(n // bn,),
        in_specs=[pl.BlockSpec((8, D_MODEL), lambda j: (0, 0)),
                  pl.BlockSpec((D_MODEL, bn), lambda j: (0, j)),
                  pl.BlockSpec((1, bn), lambda j: (0, j))],
        out_specs=pl.BlockSpec((8, bn), lambda j: (0, j)),
        out_shape=jax.ShapeDtypeStruct((8, n), jnp.float32),
        name="mod",
    )(c8, w_ada, b_ada.reshape(1, n))


def _ctx_kv_kernel(ctx_ref, mod_ref, g_ref, wk_ref, wv_ref, k_ref, v_ref):
    mod = mod_ref[0]
    gain = g_ref[...] * (1.0 + mod[:, D_MODEL:2 * D_MODEL])
    b, l, d = ctx_ref.shape
    h = _bf16(_rms(ctx_ref[...].reshape(b * l, d), gain) + mod[:, :D_MODEL])
    k_ref[...] = _bf16(_dot(h, wk_ref[...])).reshape(b, l, ATTN_WIDTH)
    v_ref[...] = _bf16(_dot(h, wv_ref[...])).reshape(b, l, ATTN_WIDTH)


def _ctx_kv_call(ctx, mod_ctx, g_pre, w_in):
    b, l, d = ctx.shape
    out = jax.ShapeDtypeStruct((b, l, ATTN_WIDTH), jnp.bfloat16)
    return pl.pallas_call(
        _ctx_kv_kernel,
        grid=(1,),
        in_specs=[pl.BlockSpec((b, l, d), lambda i: (0, 0, 0)),
                  pl.BlockSpec((1, 1, 2 * d), lambda i: (0, 0, 0)),
                  pl.BlockSpec((1, d), lambda i: (0, 0)),
                  pl.BlockSpec((d, ATTN_WIDTH), lambda i: (0, 1)),
                  pl.BlockSpec((d, ATTN_WIDTH), lambda i: (0, 2))],
        out_specs=[pl.BlockSpec((b, l, ATTN_WIDTH), lambda i: (0, 0, 0))] * 2,
        out_shape=[out, out],
        name="ctx_kv",
    )(ctx, mod_ctx, g_pre, w_in, w_in)


def _rope(a, cos, sin_signed, first_half):
    swapped = jnp.where(first_half, pltpu.roll(a, LANES - ROPE_FREQS, 1), pltpu.roll(a, ROPE_FREQS, 1))
    return a * cos + swapped * sin_signed


def _window_means_minus_token(ext, tok0, seq):
    halo = POOL_HALO
    n_ext = ext.shape[0]
    n_tok = n_ext - 2 * halo
    tok_head = tok0 + jax.lax.broadcasted_iota(jnp.int32, (halo, POOL_GROUP), 0)
    tok_tail = tok_head + (n_tok - halo)
    pooled = []
    for g, w in enumerate(POOL_WINDOWS):
        e = ext[:, g * POOL_GROUP:(g + 1) * POOL_GROUP]
        acc = e
        span = 1
        while 2 * span < w:
            acc = acc + pltpu.roll(acc, n_ext - span, 0)
            span *= 2
        acc = acc + pltpu.roll(acc, span, 0)

        def window_count(tok):
            return (jnp.minimum(tok + (w - w // 2), seq) - jnp.maximum(tok - w // 2, 0)).astype(jnp.float32)

        mean = jnp.concatenate([acc[halo:2 * halo] / window_count(tok_head),
                                acc[2 * halo:n_tok] * (1.0 / w),
                                acc[n_tok:halo + n_tok] / window_count(tok_tail)], axis=0)
        pooled.append(mean - e[halo:halo + n_tok])
    return jnp.concatenate(pooled, axis=-1)


def _proj_kernel(x_ref, xprev_ref, xnext_ref, mod_ref, g_ref, w_ref, rope_ref,
                 q_ref, k_ref, v_ref, p_ref, *, seq):
    t = pl.program_id(1)
    tile = x_ref.shape[1]
    n_sub = tile // PROJ_SUB_ROWS
    halo = POOL_HALO
    mod = mod_ref[0]
    gain = g_ref[...] * (1.0 + mod[:, D_MODEL:2 * D_MODEL])
    shift = mod[:, :D_MODEL]
    lane = jax.lax.broadcasted_iota(jnp.int32, (PROJ_SUB_ROWS, LANES), 1)
    first_half = (lane % (2 * ROPE_FREQS)) < ROPE_FREQS
    hs = {}

    def pool_part(k):
        r = k * PROJ_SUB_ROWS
        hs[k] = _bf16(_rms(x_ref[0, r:r + PROJ_SUB_ROWS, :], gain) + shift)
        x_prev = xprev_ref[0, 0] if k == 0 else x_ref[0, r - halo:r, :]
        x_next = xnext_ref[0, 0] if k == n_sub - 1 else x_ref[0, r + PROJ_SUB_ROWS:r + PROJ_SUB_ROWS + halo, :]
        h_halo = _bf16(_rms(jnp.concatenate([x_prev, x_next], axis=0), gain) + shift)
        p_all = _dot(jnp.concatenate([hs[k], h_halo], axis=0), w_ref[:, 3 * ATTN_WIDTH:])
        p_prev = p_all[PROJ_SUB_ROWS:PROJ_SUB_ROWS + halo]
        p_next = p_all[PROJ_SUB_ROWS + halo:]
        if k == 0:
            p_prev = jnp.where(t == 0, 0.0, p_prev)
        if k == n_sub - 1:
            p_next = jnp.where(t == pl.num_programs(1) - 1, 0.0, p_next)
        ext = jnp.concatenate([p_prev, p_all[:PROJ_SUB_ROWS], p_next], axis=0)
        p_ref[0, r:r + PROJ_SUB_ROWS, :] = _bf16(_window_means_minus_token(ext, t * tile + r, seq))

    def qkv_part(k):
        r = k * PROJ_SUB_ROWS
        rows = slice(r, r + PROJ_SUB_ROWS)
        proj = _dot(hs.pop(k), w_ref[:, :3 * ATTN_WIDTH])
        cos_q, sin_q, cos_k, sin_k = (rope_ref[i, rows, :] for i in range(4))
        for j in range(N_HEAD_PAIRS):
            sl = slice(j * LANES, (j + 1) * LANES)
            q_ref[0, rows, sl] = _bf16(_rope(proj[:, sl], cos_q, sin_q, first_half))
            ksl = slice(ATTN_WIDTH + j * LANES, ATTN_WIDTH + (j + 1) * LANES)
            k_ref[0, rows, sl] = _bf16(_rope(proj[:, ksl], cos_k, sin_k, first_half))
        v_ref[0, rows, :] = _bf16(proj[:, 2 * ATTN_WIDTH:3 * ATTN_WIDTH])

    for k in range(n_sub):
        pool_part(k)
        qkv_part(k)


def _proj_call(x, mod3, g_pre, w_in, rope, tile):
    b, s, d = x.shape
    n = 3 * ATTN_WIDTH + POOL_WIDTH
    halo = POOL_HALO
    assert POOL_WIDTH == ATTN_WIDTH and 2 * halo == BF16_SUBLANES and tile % BF16_SUBLANES == 0
    x4 = x.reshape(b, s // halo, halo, d)
    n_halo = s // halo
    per = tile // halo
    act = jax.ShapeDtypeStruct((b, s, ATTN_WIDTH), jnp.bfloat16)
    act_spec = pl.BlockSpec((1, tile, ATTN_WIDTH), lambda i, t: (i, t, 0))
    return pl.pallas_call(
        functools.partial(_proj_kernel, seq=s),
        grid=(b, s // tile),
        in_specs=[pl.BlockSpec((1, tile, d), lambda i, t: (i, t, 0)),
                  pl.BlockSpec((1, 1, halo, d), lambda i, t: (i, jnp.maximum(t * per - 1, 0), 0, 0)),
                  pl.BlockSpec((1, 1, halo, d), lambda i, t: (i, jnp.minimum((t + 1) * per, n_halo - 1), 0, 0)),
                  pl.BlockSpec((1, 1, N_MOD * d), lambda i, t: (i, 0, 0)),
                  pl.BlockSpec((1, d), lambda i, t: (0, 0)),
                  pl.BlockSpec((d, n), lambda i, t: (0, 0)),
                  pl.BlockSpec((4, tile, LANES), lambda i, t: (0, t, 0))],
        out_specs=[act_spec] * 4,
        out_shape=[act] * 4,
        compiler_params=pltpu.CompilerParams(vmem_limit_bytes=VMEM_LIMIT),
        name="proj",
    )(x, x4, x4, mod3, g_pre, w_in, rope)


def _band_start(i, rows):
    return jnp.clip(i * Q_ROWS - WIN_ROWS // 2, 0, rows - BAND_ROWS)


def _key_col0(n):
    return min(max(n * Q_COL_BLOCK - WIN_COLS // 2, 0), GRID_W - K_COL_BLOCK)


def _attn_kernel(q_ref, k_ref, v_ref, kc_ref, vc_ref, bias_ref, o_ref, *, rows):
    i = pl.program_id(1)
    start = pl.multiple_of(_band_start(i, rows) * GRID_W, GRID_W)
    n_q = Q_ROWS * Q_COL_BLOCK
    n_k = BAND_ROWS * K_COL_BLOCK
    n_l = HEAD_PAIR * n_q
    n_ctx = kc_ref.shape[1]
    first_d = jax.lax.broadcasted_iota(jnp.int32, (LANES, n_q), 0) < HEAD_DIM
    zero = jnp.zeros((), jnp.bfloat16)
    ones_rows = jnp.ones((BF16_SUBLANES, VAL_CHUNK), jnp.bfloat16)
    per_val = VAL_CHUNK // KEY_CHUNK
    n_key_chunks = (n_k + n_ctx) // KEY_CHUNK
    units = [(hp, n) for hp in range(N_HEAD_PAIRS) for n in range(N_COL_BLOCKS)]
    state = [dict() for _ in units]

    def band_tile(ref, hp, n):
        c0 = _key_col0(n)
        band = ref[0, pl.ds(start, BAND_ROWS * GRID_W), hp * LANES:(hp + 1) * LANES]
        return band.reshape(BAND_ROWS, GRID_W, LANES)[:, c0:c0 + K_COL_BLOCK, :].reshape(n_k, LANES)

    def score_chunk(u, j):
        hp, n = units[u]
        st = state[u]
        if j == 0:
            q_t = q_ref[0, :, hp * LANES:(hp + 1) * LANES].reshape(Q_ROWS, GRID_W, LANES)
            q_t = q_t[:, n * Q_COL_BLOCK:(n + 1) * Q_COL_BLOCK, :].reshape(n_q, LANES).T
            st["q"] = jnp.concatenate([jnp.where(first_d, q_t, zero), jnp.where(first_d, zero, q_t)], axis=1)
            st["k"] = band_tile(k_ref, hp, n)
            st["s"] = []
        lo = j * KEY_CHUNK
        if lo < n_k:
            s_j = _dot(st["k"][lo:lo + KEY_CHUNK], st["q"]) + bias_ref[0, hp, COL_CLASS[n], lo:lo + KEY_CHUNK, :]
        else:
            s_j = _dot(kc_ref[0, lo - n_k:lo - n_k + KEY_CHUNK, hp * LANES:(hp + 1) * LANES], st["q"])
        st["s"].append(s_j)
        m_j = jnp.max(s_j.reshape(KEY_CHUNK // 8, 8, n_l), axis=0)
        st["m"] = m_j if j == 0 else jnp.maximum(st["m"], m_j)

    def value_chunk(u, jv):
        hp, n = units[u]
        st = state[u]
        if jv == 0:
            st["m"] = jnp.max(st["m"], axis=0, keepdims=True)
            st["v"] = band_tile(v_ref, hp, n)
        p_v = [jnp.concatenate([_bf16(jnp.exp2(st["s"][j][:, h0:h0 + n_q] - st["m"][:, h0:h0 + n_q]))
                                for h0 in range(0, n_l, n_q)], axis=1)
               for j in range(jv * per_val, (jv + 1) * per_val)]
        lo = jv * VAL_CHUNK
        if lo < n_k:
            v_j = st["v"][lo:lo + VAL_CHUNK]
        else:
            v_j = vc_ref[0, lo - n_k:lo - n_k + VAL_CHUNK, hp * LANES:(hp + 1) * LANES]
        v_ext = jnp.concatenate([v_j.T, ones_rows], axis=0)
        o_j = _dot(v_ext, jnp.concatenate(p_v, axis=0))
        st["o"] = o_j if jv == 0 else st["o"] + o_j

    def finish(u):
        hp, n = units[u]
        st = state[u]
        o_t = st["o"]
        inv_l = 1.0 / o_t[LANES:LANES + 1, :]
        o_t = jnp.concatenate([o_t[:HEAD_DIM, :n_q] * inv_l[:, :n_q],
                               o_t[HEAD_DIM:LANES, n_q:] * inv_l[:, n_q:]], axis=0)
        o_ref[0, :, n * Q_COL_BLOCK:(n + 1) * Q_COL_BLOCK, hp * LANES:(hp + 1) * LANES] = (
            _bf16(o_t.T).reshape(Q_ROWS, Q_COL_BLOCK, LANES))
        st.clear()

    n_val_chunks = n_key_chunks // per_val
    n_slots = len(units) * n_val_chunks
    for slot in range(n_slots + n_val_chunks + VALUE_LAG):
        if slot < n_slots:
            u, jv = divmod(slot, n_val_chunks)
            for j in range(jv * per_val, (jv + 1) * per_val):
                score_chunk(u, j)
        done = slot - n_val_chunks - VALUE_LAG
        if done >= 0:
            u, jv = divmod(done, n_val_chunks)
            value_chunk(u, jv)
            if jv == n_val_chunks - 1:
                finish(u)


def _attn_call(q, k, v, k_c, v_c, bias):
    b, s, _ = q.shape
    rows = s // GRID_W
    n_i = rows // Q_ROWS
    l = k_c.shape[1]
    tok = Q_ROWS * GRID_W

    def row_class(i):
        return jnp.where(i == 0, 0, jnp.where(i == n_i - 1, 2, 1))

    assert (BAND_ROWS * K_COL_BLOCK) % VAL_CHUNK == 0 and l % VAL_CHUNK == 0 and VAL_CHUNK % KEY_CHUNK == 0
    whole_seq = pl.Buffered(2)
    return pl.pallas_call(
        functools.partial(_attn_kernel, rows=rows),
        grid=(b, n_i),
        in_specs=[pl.BlockSpec((1, tok, ATTN_WIDTH), lambda bi, i: (bi, i, 0)),
                  pl.BlockSpec((1, s, ATTN_WIDTH), lambda bi, i: (bi, 0, 0), pipeline_mode=whole_seq),
                  pl.BlockSpec((1, s, ATTN_WIDTH), lambda bi, i: (bi, 0, 0), pipeline_mode=whole_seq),
                  pl.BlockSpec((1, l, ATTN_WIDTH), lambda bi, i: (bi, 0, 0)),
                  pl.BlockSpec((1, l, ATTN_WIDTH), lambda bi, i: (bi, 0, 0)),
                  pl.BlockSpec((1, N_HEAD_PAIRS, N_COL_CLASSES, BAND_ROWS * K_COL_BLOCK,
                                HEAD_PAIR * Q_ROWS * Q_COL_BLOCK),
                               lambda bi, i: (row_class(i), 0, 0, 0, 0))],
        out_specs=pl.BlockSpec((1, Q_ROWS, GRID_W, ATTN_WIDTH), lambda bi, i: (bi, i, 0, 0)),
        out_shape=jax.ShapeDtypeStruct((b, rows, GRID_W, ATTN_WIDTH), jnp.bfloat16),
        compiler_params=pltpu.CompilerParams(vmem_limit_bytes=VMEM_LIMIT),
        name="attn",
    )(q, k, v, k_c, v_c, bias)


N_RPB_ROWS = 2 * WIN_ROWS - 1
N_RPB_COLS = 2 * WIN_COLS - 1
ROW_CLASS_STEPS = (0, 1, -1)


def _row_class_geometry(i, rows):
    start = min(max(i * Q_ROWS - WIN_ROWS // 2, 0), rows - BAND_ROWS)
    r0 = [min(max(i * Q_ROWS + a - WIN_ROWS // 2, 0), rows - WIN_ROWS) - start for a in range(Q_ROWS)]
    return start, start - i * Q_ROWS + WIN_ROWS - 1, r0


def _bias_kernel(rpb_ref, o_ref, toep_ref, *, rows):
    hp = pl.program_id(0)
    col_class = pl.program_id(1)
    n = jnp.where(col_class == 0, 0, jnp.where(col_class == N_COL_CLASSES - 1, N_COL_BLOCKS - 1, 1))
    c0 = jnp.clip(n * Q_COL_BLOCK - WIN_COLS // 2, 0, GRID_W - K_COL_BLOCK)
    kc = c0 + jax.lax.broadcasted_iota(jnp.int32, (K_COL_BLOCK, LANES), 0)
    lane = jax.lax.broadcasted_iota(jnp.int32, (K_COL_BLOCK, LANES), 1)
    q_row = lane // Q_COL_BLOCK
    qc = n * Q_COL_BLOCK + lane % Q_COL_BLOCK
    q_col0 = jnp.clip(qc - WIN_COLS // 2, 0, GRID_W - WIN_COLS)
    col_valid = (kc >= q_col0) & (kc < q_col0 + WIN_COLS)
    cidx = jnp.clip(kc - qc + WIN_COLS - 1, 0, N_RPB_COLS - 1)
    masked = jnp.full((K_COL_BLOCK, LANES), NEG_INF, jnp.float32)
    n_i = rows // Q_ROWS
    for e in range(HEAD_PAIR):
        base = (hp * HEAD_PAIR + e) * N_RPB_ROWS
        for ri in range(N_RPB_ROWS):
            rpb_row = jnp.broadcast_to(rpb_ref[pl.ds(base + ri, 1), :], (K_COL_BLOCK, LANES))
            picked = jnp.take_along_axis(rpb_row, cidx, axis=1)
            toep_ref[ri] = jnp.where(col_valid, picked * LOG2E, masked)
        for rc, step in enumerate(ROW_CLASS_STEPS):
            _, ridx0, r0 = _row_class_geometry(step % n_i, rows)
            for ar in range(BAND_ROWS):
                slab = masked
                for a in range(Q_ROWS):
                    if 0 <= ar - r0[a] < WIN_ROWS:
                        slab = jnp.where(q_row == a, toep_ref[ar - a + ridx0], slab)
                o_ref[rc, 0, 0, ar * K_COL_BLOCK:(ar + 1) * K_COL_BLOCK, e * LANES:(e + 1) * LANES] = slab


def _attn_bias_table(rpb, rows):
    assert rows // Q_ROWS >= 3 and rows % Q_ROWS == 0
    assert Q_ROWS * Q_COL_BLOCK == LANES
    for n in range(1, N_COL_BLOCKS - 1):
        assert _key_col0(n) == n * Q_COL_BLOCK - WIN_COLS // 2
        assert n * Q_COL_BLOCK >= WIN_COLS // 2 and (n + 1) * Q_COL_BLOCK - 1 - WIN_COLS // 2 <= GRID_W - WIN_COLS
    n_q, n_k = HEAD_PAIR * Q_ROWS * Q_COL_BLOCK, BAND_ROWS * K_COL_BLOCK
    n_rc = len(ROW_CLASS_STEPS)
    return pl.pallas_call(
        functools.partial(_bias_kernel, rows=rows),
        grid=(N_HEAD_PAIRS, N_COL_CLASSES),
        in_specs=[pl.BlockSpec((N_HEADS * N_RPB_ROWS, LANES), lambda hp, n: (0, 0))],
        out_specs=pl.BlockSpec((n_rc, 1, 1, n_k, n_q), lambda hp, n: (0, hp, n, 0, 0)),
        out_shape=jax.ShapeDtypeStruct((n_rc, N_HEAD_PAIRS, N_COL_CLASSES, n_k, n_q), jnp.float32),
        scratch_shapes=[pltpu.VMEM((N_RPB_ROWS, K_COL_BLOCK, LANES), jnp.float32)],
        name="bias",
    )(jnp.pad(rpb.reshape(N_HEADS * N_RPB_ROWS, N_RPB_COLS), ((0, 0), (0, LANES - N_RPB_COLS))))


def _pool_fold_kernel(pw_ref, ps_ref, wb_ref, o_ref):
    o_ref[...] = _bf16(jnp.dot(pw_ref[0] * ps_ref[0], wb_ref[...], precision=jax.lax.Precision.HIGHEST,
                               preferred_element_type=jnp.float32))


def _pool_fold_call(pool_w, pool_scale, w_proj_b):
    d = w_proj_b.shape[1]
    return pl.pallas_call(
        _pool_fold_kernel,
        grid=(N_POOL_GROUPS,),
        in_specs=[pl.BlockSpec((1, POOL_GROUP, POOL_GROUP), lambda g: (g, 0, 0)),
                  pl.BlockSpec((1, 1, POOL_GROUP), lambda g: (g, 0, 0)),
                  pl.BlockSpec((POOL_GROUP, d), lambda g: (g, 0))],
        out_specs=pl.BlockSpec((POOL_GROUP, d), lambda g: (g, 0)),
        out_shape=jax.ShapeDtypeStruct((POOL_WIDTH, d), jnp.bfloat16),
        name="pool_fold",
    )(pool_w, pool_scale.reshape(N_POOL_GROUPS, 1, POOL_GROUP), w_proj_b)


def _merge_kernel(x_ref, mod_ref, gpre_ref, gpost_ref, attn_ref, p_ref,
                  wg_ref, bg_ref, wa_ref, wb_ref, wo_ref, o_ref):
    n_sub = x_ref.shape[1] // SUB_ROWS
    mod = mod_ref[0]
    g_in = gpre_ref[...] * (1.0 + mod[:, D_MODEL:2 * D_MODEL])
    g_out = gpost_ref[...] * mod[:, 2 * D_MODEL:3 * D_MODEL]
    n_chunks = D_MODEL // GATE_CHUNK
    lead = min(3, n_chunks)
    state = [dict() for _ in range(n_sub)]

    def gate_pair(st, c):
        lo = c * GATE_CHUNK
        g_a = _sigmoid(_dot(st["h"], wg_ref[:, lo:lo + GATE_CHUNK]) + bg_ref[:, lo:lo + GATE_CHUNK])
        lo += D_MODEL
        g_b = _sigmoid(_dot(st["h"], wg_ref[:, lo:lo + GATE_CHUNK]) + bg_ref[:, lo:lo + GATE_CHUNK])
        st["gates"].append((g_a, g_b))

    def out_partial(st, c):
        g_a, g_b = st["gates"][c]
        sl = slice(c * GATE_CHUNK, (c + 1) * GATE_CHUNK)
        part = _dot(_bf16(g_a * st["ya"][:, sl] + g_b * st["yb"][:, sl]), wo_ref[sl, :])
        st["y"] = part if c == 0 else st["y"] + part

    def stage(k, s):
        st = state[k]
        rows = slice(k * SUB_ROWS, (k + 1) * SUB_ROWS)
        if s == 0:
            st["ya"] = _dot(attn_ref[0, rows, :], wa_ref[...])
            st["yb"] = _dot(p_ref[0, rows, :], wb_ref[...])
        elif s <= lead:
            if s == 1:
                st["h"] = _bf16(_rms(x_ref[0, rows, :], g_in) + mod[:, :D_MODEL])
                st["gates"] = []
            gate_pair(st, s - 1)
        else:
            c = s - lead - 1
            if c + lead < n_chunks:
                gate_pair(st, c + lead)
            out_partial(st, c)
            if c == n_chunks - 1:
                o_ref[0, rows, :] = x_ref[0, rows, :] + _rms(st["y"], g_out)
                st.clear()

    n_stages = lead + 1 + n_chunks
    for step in range(n_stages + MERGE_LAG * (n_sub - 1)):
        for k in range(n_sub):
            s = step - k * MERGE_LAG
            if 0 <= s < n_stages:
                stage(k, s)


def _merge_call(x, mod3, g_pre, g_post, attn, pooled, w_in, b_gate, w_a, w_b, w_o, tile):
    b, s, d = x.shape
    assert w_in.shape[1] == 2 * GATE_WIDTH

    def const(shape):
        return pl.BlockSpec(shape, lambda i, t: (0,) * len(shape))

    return pl.pallas_call(
        _merge_kernel,
        grid=(b, s // tile),
        in_specs=[pl.BlockSpec((1, tile, d), lambda i, t: (i, t, 0)),
                  pl.BlockSpec((1, 1, N_MOD * d), lambda i, t: (i, 0, 0)),
                  const((1, d)), const((1, d)),
                  pl.BlockSpec((1, tile, ATTN_WIDTH), lambda i, t: (i, t, 0)),
                  pl.BlockSpec((1, tile, POOL_WIDTH), lambda i, t: (i, t, 0)),
                  pl.BlockSpec((d, GATE_WIDTH), lambda i, t: (0, 1)), const((1, GATE_WIDTH)),
                  const((ATTN_WIDTH, d)), const((POOL_WIDTH, d)), const((d, d))],
        out_specs=pl.BlockSpec((1, tile, d), lambda i, t: (i, t, 0)),
        out_shape=jax.ShapeDtypeStruct((b, s, d), jnp.float32),
        compiler_params=pltpu.CompilerParams(vmem_limit_bytes=VMEM_LIMIT),
        name="merge",
    )(x, mod3, g_pre, g_post, attn, pooled, w_in, b_gate, w_a, w_b, w_o)


def _ffn_kernel(x_ref, mod_ref, gpre_ref, gpost_ref, wu_ref, wd_ref, o_ref, *, d_ff, chunk):
    mod = mod_ref[0]
    g_in = gpre_ref[...] * (1.0 + mod[:, 4 * D_MODEL:5 * D_MODEL])
    g_out = gpost_ref[...] * mod[:, 5 * D_MODEL:]
    subs = list(range(0, x_ref.shape[1], SUB_ROWS))
    n_chunks = d_ff // chunk
    lag = n_chunks // 2
    hs, accs = {}, {}

    def ffn_chunk(r, c):
        if c == 0:
            hs[r] = _bf16(_rms(x_ref[0, r:r + SUB_ROWS, :], g_in) + mod[:, 3 * D_MODEL:4 * D_MODEL])
        c0 = c * chunk
        gate = _dot(hs[r], _bf16(wu_ref[:, c0:c0 + chunk]))
        up = _dot(hs[r], _bf16(wu_ref[:, d_ff + c0:d_ff + c0 + chunk]))
        part = _dot(_bf16(gate * _sigmoid(gate) * up), wd_ref[c0:c0 + chunk, :])
        accs[r] = part if c == 0 else accs[r] + part
        if c == n_chunks - 1:
            o_ref[0, r:r + SUB_ROWS, :] = x_ref[0, r:r + SUB_ROWS, :] + _rms(accs[r], g_out)

    for step in range(n_chunks + lag * (len(subs) - 1)):
        for idx, r in enumerate(subs):
            c = step - idx * lag
            if 0 <= c < n_chunks:
                ffn_chunk(r, c)


def _ffn_call(x, mod3, g_pre, g_post, w_up, w_down, tile):
    b, s, d = x.shape
    d_ff = w_down.shape[0]
    chunk = 2 * LANES
    assert d_ff % chunk == 0

    def const(shape):
        return pl.BlockSpec(shape, lambda i, t: (0,) * len(shape))

    return pl.pallas_call(
        functools.partial(_ffn_kernel, d_ff=d_ff, chunk=chunk),
        grid=(b, s // tile),
        in_specs=[pl.BlockSpec((1, tile, d), lambda i, t: (i, t, 0)),
                  pl.BlockSpec((1, 1, N_MOD * d), lambda i, t: (i, 0, 0)),
                  const((1, d)), const((1, d)),
                  pl.BlockSpec((d, 2 * d_ff), lambda i, t: (0, 0), pipeline_mode=pl.Buffered(1)),
                  pl.BlockSpec((d_ff, d), lambda i, t: (0, 0), pipeline_mode=pl.Buffered(1))],
        out_specs=pl.BlockSpec((1, tile, d), lambda i, t: (i, t, 0)),
        out_shape=jax.ShapeDtypeStruct((b, s, d), jnp.float32),
        compiler_params=pltpu.CompilerParams(vmem_limit_bytes=VMEM_LIMIT),
        name="ffn",
    )(x, mod3, g_pre, g_post, w_up, w_down)


def _rope_tables(n_tok):
    t = np.arange(n_tok)
    pos = np.stack([t // GRID_W, t % GRID_W], axis=-1).astype(np.float64)
    inv = ROPE_THETA ** (-np.arange(ROPE_FREQS, dtype=np.float64) / ROPE_FREQS)
    ang = pos[:, :, None] * inv
    cos, sin = np.cos(ang), np.sin(ang)
    cos_h = np.concatenate([cos[:, 0], cos[:, 0], cos[:, 1], cos[:, 1]], axis=-1)
    sin_h = np.concatenate([-sin[:, 0], sin[:, 0], -sin[:, 1], sin[:, 1]], axis=-1)
    cos_p, sin_p = np.tile(cos_h, (1, HEAD_PAIR)), np.tile(sin_h, (1, HEAD_PAIR))
    q_scale = HEAD_DIM ** -0.5 * LOG2E
    return jnp.asarray(np.stack([cos_p * q_scale, sin_p * q_scale, cos_p, sin_p]).astype(np.float32))


def kernel(x, c, ctx, c_ctx, w_ada, b_ada, g_pre_mix, g_post_mix, g_pre_ffn, g_post_ffn, w_in, b_gate, rpb,
           pool_w, pool_scale, w_proj_a, w_proj_b, w_out, w_up, w_down):
    b, s, d = x.shape
    depth = w_ada.shape[0]
    assert depth == 1, "context stream updates are only needed for depth > 1"
    rows = s // GRID_W
    a = ATTN_WIDTH

    c8 = jnp.zeros((8, d), jnp.float32).at[:b].set(c).at[b].set(c_ctx)
    mod = _mod_call(c8, w_ada[0], b_ada[0])
    mod3 = mod[:b].reshape(b, 1, N_MOD * d)
    mod_ctx = mod[b:b + 1, :2 * d].reshape(1, 1, 2 * d)

    w_in_b = _bf16(w_in[0])
    g_pre = g_pre_mix[0].reshape(1, d)
    k_c, v_c = _ctx_kv_call(ctx, mod_ctx, g_pre, w_in_b)
    q, k, v, pooled = _proj_call(x, mod3, g_pre, w_in_b, _rope_tables(s), tile=PROJ_TILE)
    bias = _attn_bias_table(rpb[0], rows)
    attn = _attn_call(q, k, v, k_c, v_c, bias).reshape(b, s, a)
    w_pool = _pool_fold_call(pool_w[0], pool_scale[0], w_proj_b[0])
    x1 = _merge_call(x, mod3, g_pre, g_post_mix[0].reshape(1, d), attn, pooled, w_in_b, b_gate[0].reshape(1, -1),
                     _bf16(w_proj_a[0]), w_pool, _bf16(w_out[0]), tile=2 * SUB_ROWS)
    return _ffn_call(x1, mod3, g_pre_ffn[0].reshape(1, d), g_post_ffn[0].reshape(1, d),
                     w_up[0], _bf16(w_down[0]), tile=2 * SUB_ROWS)
```
